```python
import math
import jax, jax.numpy as jnp
from jax import lax
import numpy as np

D_MODEL = 1024
BATCH = 8
SEQ = 2048
DEPTH = 1
DEC_BATCH = 4
DEC_SEQ = 8192
PAST_LEN = 128

MLA_HEADS = 16
QK_NOPE = 64
QK_ROPE = 32
V_DIM = 64
Q_LORA = 384
KV_LORA = 256
ROPE_THETA = 10000.0
DIL_CONFIGS = ((128, 1), (512, 4), (2048, 16))
DIL_HEADS_PER_GROUP = 4
DIL_HEADS = DIL_HEADS_PER_GROUP * len(DIL_CONFIGS)
DIL_HEAD_DIM = 64
D_FF = 2816
CONV_WIDTH = 3
Q_BLOCK = 128
EPS = 1e-6
IN_SPLITS = (Q_LORA, KV_LORA, QK_ROPE, 3 * DIL_HEADS * DIL_HEAD_DIM, D_MODEL, D_MODEL)
IN_COLS = Q_LORA + KV_LORA + QK_ROPE + 3 * DIL_HEADS * DIL_HEAD_DIM + 2 * D_MODEL

kernel_name = 'hybrid_mla_dilated_convffn_adaln_encoder'


def rmsnorm(x, g):
    xf = x.astype(jnp.float32)
    y = xf * lax.rsqrt(jnp.mean(xf * xf, axis=-1, keepdims=True) + EPS)
    return (y * g.astype(jnp.float32)).astype(x.dtype)


def rope_cos_sin(seq_len):
    inv = ROPE_THETA ** (-jnp.arange(0, QK_ROPE, 2, dtype=jnp.float32) / QK_ROPE)
    ang = jnp.arange(seq_len, dtype=jnp.float32)[:, None] * inv[None, :]
    return jnp.cos(ang), jnp.sin(ang)


def apply_rope(x, cos, sin):
    cos = cos.astype(x.dtype)
    sin = sin.astype(x.dtype)
    x1, x2 = jnp.split(x, 2, axis=-1)
    return jnp.concatenate([x1 * cos - x2 * sin, x1 * sin + x2 * cos], axis=-1)


def mla_attention(q_nope, q_rope, k_nope, k_rope, v):
    B, S, H, _ = q_nope.shape
    nb = S // Q_BLOCK
    scale = (QK_NOPE + QK_ROPE) ** -0.5
    qn = q_nope.reshape(B, nb, Q_BLOCK, H, QK_NOPE).transpose(1, 0, 2, 3, 4)
    qr = q_rope.reshape(B, nb, Q_BLOCK, H, QK_ROPE).transpose(1, 0, 2, 3, 4)

    def block(args):
        qn_b, qr_b = args
        s = (jnp.einsum('bqhd,bkhd->bhqk', qn_b, k_nope)
             + jnp.einsum('bqhr,bkr->bhqk', qr_b, k_rope)).astype(jnp.float32) * scale
        p = jax.nn.softmax(s, axis=-1)
        return jnp.einsum('bhqk,bkhd->bqhd', p.astype(v.dtype), v)

    o = lax.map(block, (qn, qr))
    return o.transpose(1, 0, 2, 3, 4).reshape(B, S, H * V_DIM)


def dilated_group_attention(q, k, v, slopes, window, dilation):
    B, S, Hg, dh = q.shape
    n_side = (window // 2) // dilation
    pad = n_side * dilation
    offsets = jnp.arange(-n_side, n_side + 1) * dilation
    kp = jnp.pad(k, ((0, 0), (pad, pad), (0, 0), (0, 0)))
    vp = jnp.pad(v, ((0, 0), (pad, pad), (0, 0), (0, 0)))
    alibi = -slopes[:, None] * jnp.abs(offsets).astype(jnp.float32)[None, :]
    scale = dh ** -0.5
    nb = S // Q_BLOCK
    qb = q.reshape(B, nb, Q_BLOCK, Hg, dh).transpose(1, 0, 2, 3, 4)
    starts = jnp.arange(nb) * Q_BLOCK

    def block(args):
        q_b, t0 = args
        pos = t0 + jnp.arange(Q_BLOCK)[:, None] + offsets[None, :]
        valid = (pos >= 0) & (pos < S)
        idx = pos + pad
        k_g = jnp.take(kp, idx, axis=1)
        v_g = jnp.take(vp, idx, axis=1)
        s = jnp.einsum('bqhd,bqjhd->bhqj', q_b, k_g).astype(jnp.float32) * scale + alibi[None, :, None, :]
        s = jnp.where(valid[None, None], s, -jnp.inf)
        lse = jax.nn.logsumexp(s, axis=-1)
        p = jnp.exp(s - lse[..., None])
        o = jnp.einsum('bhqj,bqjhd->bqhd', p.astype(v.dtype), v_g)
        return o, lse.transpose(0, 2, 1)

    o, lse = lax.map(block, (qb, starts))
    o = o.transpose(1, 0, 2, 3, 4).reshape(B, S, Hg, dh)
    lse = lse.transpose(1, 0, 2, 3).reshape(B, S, Hg)
    return o, lse


def dwconv_centered(u, w, b):
    S = u.shape[1]
    half = CONV_WIDTH // 2
    up = jnp.pad(u, ((0, 0), (half, half), (0, 0)))
    out = b
    for j in range(CONV_WIDTH):
        out = out + up[:, j:j + S] * w[j]
    return out


def encoder_layer(x, c, ada_w, ada_b, norm1_g, w_in, q_norm_g, kv_norm_g, w_uq, w_ukv,
                  p_a, p_b, w_out, norm2_g, w_up, conv_w, conv_b, w_down):
    B, S, D = x.shape
    mod = jnp.einsum('bd,de->be', jax.nn.silu(c), ada_w) + ada_b
    sh1, sc1, gt1, sh2, sc2, gt2 = jnp.split(mod, 6, axis=-1)

    h = rmsnorm(x, norm1_g) * (1 + sc1[:, None]) + sh1[:, None]
    z = jnp.einsum('bsd,de->bse', h, w_in)
    cuts = list(np.cumsum(IN_SPLITS)[:-1])
    c_q, c_kv, k_r, qkv_d, gate_a, gate_b = jnp.split(z, cuts, axis=-1)

    q = jnp.einsum('bsr,re->bse', rmsnorm(c_q, q_norm_g), w_uq).reshape(B, S, MLA_HEADS, QK_NOPE + QK_ROPE)
    q_nope, q_rope = q[..., :QK_NOPE], q[..., QK_NOPE:]
    kv = jnp.einsum('bsr,re->bse', rmsnorm(c_kv, kv_norm_g), w_ukv).reshape(B, S, MLA_HEADS, QK_NOPE + V_DIM)
    k_nope, v_a = kv[..., :QK_NOPE], kv[..., QK_NOPE:]
    cos, sin = rope_cos_sin(S)
    q_rope = apply_rope(q_rope, cos[:, None, :], sin[:, None, :])
    k_rope = apply_rope(k_r, cos, sin)
    o_a = mla_attention(q_nope, q_rope, k_nope, k_rope, v_a)

    qkv = qkv_d.reshape(B, S, 3, DIL_HEADS, DIL_HEAD_DIM)
    slopes = 2.0 ** (-8.0 * jnp.arange(1, DIL_HEADS + 1, dtype=jnp.float32) / DIL_HEADS)
    outs, lses = [], []
    for g, (window, dilation) in enumerate(DIL_CONFIGS):
        hs = slice(g * DIL_HEADS_PER_GROUP, (g + 1) * DIL_HEADS_PER_GROUP)
        o_g, lse_g = dilated_group_attention(qkv[:, :, 0, hs], qkv[:, :, 1, hs], qkv[:, :, 2, hs],
                                             slopes[hs], window, dilation)
        outs.append(o_g)
        lses.append(lse_g)
    wts = jax.nn.softmax(jnp.stack(lses, axis=0), axis=0)
    o_b = jnp.sum(wts[..., None].astype(x.dtype) * jnp.stack(outs, axis=0), axis=0)
    o_b = o_b.reshape(B, S, DIL_HEADS_PER_GROUP * DIL_HEAD_DIM)

    g_a = jax.nn.sigmoid(gate_a.astype(jnp.float32)).astype(x.dtype)
    g_b = jax.nn.sigmoid(gate_b.astype(jnp.float32)).astype(x.dtype)
    merged = g_a * jnp.einsum('bse,ed->bsd', o_a, p_a) + g_b * jnp.einsum('bse,ed->bsd', o_b, p_b)
    x = x + gt1[:, None] * jnp.einsum('bsd,de->bse', merged, w_out)

    h2 = rmsnorm(x, norm2_g) * (1 + sc2[:, None]) + sh2[:, None]
    up = jnp.einsum('bsd,df->bsf', h2, w_up)
    u, gv = jnp.split(up, 2, axis=-1)
    u = dwconv_centered(u, conv_w, conv_b)
    ff = jnp.einsum('bsf,fd->bsd', jax.nn.gelu(u) * gv, w_down)
    x = x + gt2[:, None] * ff
    return x


def setup_inputs(seed: int = 0) -> dict:
    key = jax.random.key(seed)
    ks = jax.random.split(key, 24)
    f32 = jnp.float32

    def nrm(k, shape, fan_in):
        return jax.random.normal(k, shape, f32) * (fan_in ** -0.5)

    def gain(k, shape):
        return 1.0 + 0.05 * jax.random.normal(k, shape, f32)

    L = DEPTH
    return {
        'x_prompt': jax.random.normal(ks[0], (BATCH, SEQ, D_MODEL), f32),
        'x_sample': jax.random.normal(ks[1], (DEC_BATCH, DEC_SEQ, D_MODEL), f32),
        'c_prompt': jax.random.normal(ks[2], (BATCH, D_MODEL), f32),
        'c_sample': jax.random.normal(ks[3], (DEC_BATCH, D_MODEL), f32),
        'ada_w': nrm(ks[4], (L, D_MODEL, 6 * D_MODEL), D_MODEL),
        'ada_b': 0.02 * jax.random.normal(ks[5], (L, 6 * D_MODEL), f32),
        'norm1_g': gain(ks[6], (L, D_MODEL)),
        'w_in': nrm(ks[7], (L, D_MODEL, IN_COLS), D_MODEL),
        'q_norm_g': gain(ks[8], (L, Q_LORA)),
        'kv_norm_g': gain(ks[9], (L, KV_LORA)),
        'w_uq': nrm(ks[10], (L, Q_LORA, MLA_HEADS * (QK_NOPE + QK_ROPE)), Q_LORA),
        'w_ukv': nrm(ks[11], (L, KV_LORA, MLA_HEADS * (QK_NOPE + V_DIM)), KV_LORA),
        'p_a': nrm(ks[12], (L, MLA_HEADS * V_DIM, D_MODEL), MLA_HEADS * V_DIM),
        'p_b': nrm(ks[13], (L, DIL_HEADS_PER_GROUP * DIL_HEAD_DIM, D_MODEL), DIL_HEADS_PER_GROUP * DIL_HEAD_DIM),
        'w_out': nrm(ks[14], (L, D_MODEL, D_MODEL), D_MODEL),
        'norm2_g': gain(ks[15], (L, D_MODEL)),
        'w_up': nrm(ks[16], (L, D_MODEL, 2 * D_FF), D_MODEL),
        'conv_w': nrm(ks[17], (L, CONV_WIDTH, D_FF), CONV_WIDTH),
        'conv_b': 0.02 * jax.random.normal(ks[18], (L, D_FF), f32),
        'w_down': nrm(ks[19], (L, D_FF, D_MODEL), D_FF),
        'normf_g': gain(ks[20], (D_MODEL,)),
    }


def reference(x_prompt, x_sample, c_prompt, c_sample, ada_w, ada_b, norm1_g, w_in, q_norm_g, kv_norm_g,
              w_uq, w_ukv, p_a, p_b, w_out, norm2_g, w_up, conv_w, conv_b, w_down, normf_g):
    def trunk(x, c):
        for l in range(DEPTH):
            x = encoder_layer(x, c, ada_w[l], ada_b[l], norm1_g[l], w_in[l], q_norm_g[l], kv_norm_g[l],
                              w_uq[l], w_ukv[l], p_a[l], p_b[l], w_out[l], norm2_g[l], w_up[l],
                              conv_w[l], conv_b[l], w_down[l])
        return rmsnorm(x, normf_g)

    y_prompt = trunk(x_prompt, c_prompt)
    y_sample = trunk(x_sample, c_sample)
    return (y_prompt, y_sample)
```

```python
import functools
import math

import jax
import jax.numpy as jnp
import numpy as np
from jax import lax
from jax.experimental import pallas as pl
from jax.experimental.pallas import tpu as pltpu

F32 = jnp.float32
BF16 = jnp.bfloat16

D_MODEL = 1024
MLA_HEADS = 16
QK_NOPE = 64
QK_ROPE = 32
V_DIM = 64
Q_LORA = 384
KV_LORA = 256
ROPE_THETA = 10000.0
DIL_CONFIGS = ((128, 1), (512, 4), (2048, 16))
DIL_HEADS_PER_GROUP = 4
DIL_HEADS = DIL_HEADS_PER_GROUP * len(DIL_CONFIGS)
DIL_HEAD_DIM = 64
D_FF = 2816
EPS = 1e-6

LANES = 128
HEAD_PAD = 128
DIL_W = DIL_HEADS_PER_GROUP * DIL_HEAD_DIM
N_SIDE = 64
Q_BLK = 128
K_WIN = Q_BLK + 2 * N_SIDE
DIL_TILE = 2048
ROW_TILE = 512
FFN_HALO = 16
FF_CHUNK = 256
MLA_TQ = 512
MLA_TK = 512
NEG = -1e30
VMEM_LIMIT = 56 * 1024 * 1024

ZA_COLS = Q_LORA + KV_LORA + LANES
ZD_COLS = 3 * DIL_HEADS * DIL_HEAD_DIM


def _params(sem):
    return pltpu.CompilerParams(dimension_semantics=sem, vmem_limit_bytes=VMEM_LIMIT)


def _const_spec(shape):
    nd = len(shape)
    return pl.BlockSpec(shape, lambda *_: (0,) * nd, pipeline_mode=pl.Buffered(1))


def _rms(x, g):
    return x * lax.rsqrt(jnp.mean(x * x, axis=-1, keepdims=True) + EPS) * g


def _dot(a, b):
    return jnp.dot(a, b, preferred_element_type=F32)


def _dot_nt(a, b):
    return lax.dot_general(a, b, (((1,), (1,)), ((), ())), preferred_element_type=F32)


def _mod_kernel(c_ref, w_ref, b_ref, o_ref):
    c = c_ref[...]
    a = c * jax.nn.sigmoid(c)
    a_hi = a.astype(BF16)
    a_lo = (a - a_hi.astype(F32)).astype(BF16)
    w = w_ref[...]
    w_hi = w.astype(BF16)
    w_lo = (w - w_hi.astype(F32)).astype(BF16)
    o_ref[...] = _dot(a_hi, w_hi) + _dot(a_lo, w_hi) + _dot(a_hi, w_lo) + b_ref[...]


def _mod_call(c_all, ada_w, ada_b):
    rows = c_all.shape[0]
    n = ada_w.shape[1]
    tn = 1536
    return pl.pallas_call(
        _mod_kernel,
        grid=(n // tn,),
        in_specs=[
            pl.BlockSpec((rows, D_MODEL), lambda j: (0, 0)),
            pl.BlockSpec((D_MODEL, tn), lambda j: (0, j)),
            pl.BlockSpec((1, tn), lambda j: (0, j)),
        ],
        out_specs=pl.BlockSpec((rows, tn), lambda j: (0, j)),
        out_shape=jax.ShapeDtypeStruct((rows, n), F32),
        compiler_params=_params(("arbitrary",)),
        name="mod",
    )(c_all, ada_w, ada_b)


def _in_kernel(x_ref, mod_ref, g1_ref, wa_ref, wd_ref, qg_ref, kvg_ref, wq_ref, wk_ref, wv_ref,
               t1q_ref, t2q_ref, t1k_ref, t2k_ref, q_ref, k_ref, v_ref, zd_ref):
    x = x_ref[0]
    mod = mod_ref[0]
    h = _rms(x, g1_ref[...]) * (1.0 + mod[1:2]) + mod[0:1]
    hb = h.astype(BF16)
    zd_ref[0] = _dot(hb, wd_ref[...]).astype(BF16)
    za = _dot(hb, wa_ref[...])
    cq = _rms(za[:, :Q_LORA], qg_ref[...]).astype(BF16)
    ckv = _rms(za[:, Q_LORA:Q_LORA + KV_LORA], kvg_ref[...]).astype(BF16)
    kr = za[:, Q_LORA + KV_LORA:]
    kr = kr * t1k_ref[...] + pltpu.roll(kr, LANES - QK_ROPE, 1) * t2k_ref[...]
    v_ref[0] = _dot(ckv, wv_ref[...]).astype(BF16)
    qf = _dot(cq, wq_ref[...])
    kf = _dot(ckv, wk_ref[...])
    t1q = t1q_ref[...]
    t2q = t2q_ref[...]
    for hd in range(MLA_HEADS):
        sl = slice(hd * HEAD_PAD, (hd + 1) * HEAD_PAD)
        qh = qf[:, sl]
        q_ref[0, :, sl] = (qh * t1q + pltpu.roll(qh, LANES - QK_ROPE, 1) * t2q).astype(BF16)
        k_ref[0, :, sl] = (kf[:, sl] + kr).astype(BF16)


def _in_call(x, mod, w, tabs):
    B, S, _ = x.shape
    T = ROW_TILE
    row = lambda c: pl.BlockSpec((1, T, c), lambda b, i: (b, i, 0))
    tab = pl.BlockSpec((T, LANES), lambda b, i: (i, 0))
    qk_cols = MLA_HEADS * HEAD_PAD
    return pl.pallas_call(
        _in_kernel,
        grid=(B, S // T),
        in_specs=[
            row(D_MODEL),
            pl.BlockSpec((1, 8, D_MODEL), lambda b, i: (b, 0, 0)),
            _const_spec((1, D_MODEL)),
            _const_spec((D_MODEL, ZA_COLS)),
            _const_spec((D_MODEL, ZD_COLS)),
            _const_spec((1, Q_LORA)),
            _const_spec((1, KV_LORA)),
            _const_spec((Q_LORA, qk_cols)),
            _const_spec((KV_LORA, qk_cols)),
            _const_spec((KV_LORA, MLA_HEADS * V_DIM)),
            tab, tab, tab, tab,
        ],
        out_specs=[row(qk_cols), row(qk_cols), row(MLA_HEADS * V_DIM), row(ZD_COLS)],
        out_shape=[
            jax.ShapeDtypeStruct((B, S, qk_cols), BF16),
            jax.ShapeDtypeStruct((B, S, qk_cols), BF16),
            jax.ShapeDtypeStruct((B, S, MLA_HEADS * V_DIM), BF16),
            jax.ShapeDtypeStruct((B, S, ZD_COLS), BF16),
        ],
        compiler_params=_params(("parallel", "parallel")),
        name="in_proj",
    )(x, mod, w["g1"], w["wa"], w["wd"], w["qg"], w["kvg"], w["wq"], w["wk"], w["wv"], *tabs)


def _mla_kernel(q_ref, k_ref, v_ref, o_ref, *, n_kv):
    tq = q_ref.shape[1]
    outs = []
    for hh in range(2):
        sl = slice(hh * HEAD_PAD, (hh + 1) * HEAD_PAD)
        q = q_ref[0, :, sl]

        def body(j, carry, sl=sl, q=q):
            m, l, acc = carry
            rows = pl.ds(pl.multiple_of(j * MLA_TK, MLA_TK), MLA_TK)
            k = k_ref[0, rows, sl]
            v = v_ref[0, rows, :]
            s = _dot_nt(q, k)
            m_new = jnp.maximum(m, jnp.max(s, axis=1, keepdims=True))
            alpha = jnp.exp(m - m_new)
            p = jnp.exp(s - m_new)
            l = alpha * l + jnp.sum(p, axis=1, keepdims=True)
            acc = alpha * acc + _dot(p.astype(BF16), v)
            return m_new, l, acc

        init = (jnp.full((tq, 1), -jnp.inf, F32), jnp.zeros((tq, 1), F32), jnp.zeros((tq, 2 * V_DIM), F32))
        _, l, acc = lax.fori_loop(0, n_kv, body, init)
        outs.append(acc / l)
    lane = lax.broadcasted_iota(jnp.int32, (tq, 2 * V_DIM), 1)
    o_ref[0] = jnp.where(lane < V_DIM, outs[0], outs[1]).astype(BF16)


def _mla_call(q, k, v):
    B, S, _ = q.shape
    tq = MLA_TQ
    return pl.pallas_call(
        functools.partial(_mla_kernel, n_kv=S // MLA_TK),
        grid=(B, MLA_HEADS // 2, S // tq),
        in_specs=[
            pl.BlockSpec((1, tq, 2 * HEAD_PAD), lambda b, h, i: (b, i, h)),
            pl.BlockSpec((1, S, 2 * HEAD_PAD), lambda b, h, i: (b, 0, h)),
            pl.BlockSpec((1, S, 2 * V_DIM), lambda b, h, i: (b, 0, h)),
        ],
        out_specs=pl.BlockSpec((1, tq, 2 * V_DIM), lambda b, h, i: (b, i, h)),
        out_shape=jax.ShapeDtypeStruct((B, S, MLA_HEADS * V_DIM), BF16),
        compiler_params=_params(("parallel", "parallel", "arbitrary")),
        name="mla_attn",
    )(q, k, v)


def _dil_kernel(q_ref, kp_ref, kc_ref, kn_ref, vp_ref, vc_ref, vn_ref, o_ref, lse_ref,
                qs, ks, vs, os_, ls_, *, dil, group, seq_len):
    halo = N_SIDE * dil
    n_blk = DIL_TILE // (Q_BLK * dil)
    sc_len = seq_len // dil
    i0 = pl.program_id(1) * (DIL_TILE // dil)

    for pair in range(2):
        cs = slice(pair * LANES, (pair + 1) * LANES)
        qs[pair] = q_ref[0, :, cs].astype(F32)
        for dst, (p_ref, c_ref, n_ref) in ((ks, (kp_ref, kc_ref, kn_ref)), (vs, (vp_ref, vc_ref, vn_ref))):
            dst[pair, 0:halo] = p_ref[0, :, cs].astype(F32)
            dst[pair, halo:halo + DIL_TILE] = c_ref[0, :, cs].astype(F32)
            dst[pair, halo + DIL_TILE:] = n_ref[0, :, cs].astype(F32)

    col = lax.broadcasted_iota(jnp.int32, (Q_BLK, K_WIN), 1)
    rowi = lax.broadcasted_iota(jnp.int32, (Q_BLK, K_WIN), 0)
    rel = col - N_SIDE - rowi
    absrel = jnp.abs(rel).astype(F32)
    band = jnp.abs(rel) <= N_SIDE
    lane = lax.broadcasted_iota(jnp.int32, (Q_BLK, LANES), 1)
    first_head = lane < DIL_HEAD_DIM
    scale = DIL_HEAD_DIM ** -0.5

    for jb in range(n_blk):
        kc0 = i0 + (jb * Q_BLK - N_SIDE)
        valid = band & (col >= -kc0) & (col < sc_len - kc0)
        for r in range(dil):
            start = jb * Q_BLK * dil + r
            q_rows = pl.ds(start, Q_BLK, stride=dil) if dil > 1 else pl.ds(start, Q_BLK)
            k_rows = pl.ds(start, K_WIN, stride=dil) if dil > 1 else pl.ds(start, K_WIN)
            for pair in range(2):
                qp = qs[pair, q_rows, :] * scale
                kp = ks[pair, k_rows, :].astype(BF16)
                vp = vs[pair, k_rows, :].astype(BF16)
                o_acc = None
                lse_acc = None
                for hh in range(2):
                    head = group * DIL_HEADS_PER_GROUP + pair * 2 + hh
                    slope = 2.0 ** (-8.0 * (head + 1) / DIL_HEADS)
                    sel = first_head if hh == 0 else jnp.logical_not(first_head)
                    qm = jnp.where(sel, qp, 0.0).astype(BF16)
                    s = _dot_nt(qm, kp) - absrel * (slope * dil)
                    s = jnp.where(valid, s, NEG)
                    m = jnp.max(s, axis=1, keepdims=True)
                    p = jnp.exp(s - m)
                    l = jnp.sum(p, axis=1, keepdims=True)
                    o = _dot(p.astype(BF16), vp) / l
                    lse = jnp.broadcast_to(m + jnp.log(l), (Q_BLK, LANES))
                    o_acc = o if hh == 0 else jnp.where(sel, o, o_acc)
                    lse_acc = lse if hh == 0 else jnp.where(sel, lse, lse_acc)
                os_[pair, q_rows, :] = o_acc
                ls_[pair, q_rows, :] = lse_acc

    for pair in range(2):
        cs = slice(pair * LANES, (pair + 1) * LANES)
        o_ref[0, :, cs] = os_[pair]
        lse_ref[0, :, cs] = ls_[pair]


def _dil_call(zd, group, dil):
    B, S, _ = zd.shape
    halo = N_SIDE * dil
    per = DIL_TILE // halo
    n_halo = S // halo
    g = group
    cur = lambda c: pl.BlockSpec((1, DIL_TILE, DIL_W), lambda b, i: (b, i, c))
    prev = lambda c: pl.BlockSpec((1, halo, DIL_W), lambda b, i: (b, jnp.maximum(i * per - 1, 0), c))
    nxt = lambda c: pl.BlockSpec((1, halo, DIL_W), lambda b, i: (b, jnp.minimum((i + 1) * per, n_halo - 1), c))
    n_g = len(DIL_CONFIGS)
    out_spec = pl.BlockSpec((1, DIL_TILE, DIL_W), lambda b, i: (b, i, 0))
    return pl.pallas_call(
        functools.partial(_dil_kernel, dil=dil, group=group, seq_len=S),
        grid=(B, S // DIL_TILE),
        in_specs=[cur(g), prev(n_g + g), cur(n_g + g), nxt(n_g + g),
                  prev(2 * n_g + g), cur(2 * n_g + g), nxt(2 * n_g + g)],
        out_specs=[out_spec, out_spec],
        out_shape=[jax.ShapeDtypeStruct((B, S, DIL_W), F32), jax.ShapeDtypeStruct((B, S, DIL_W), F32)],
        scratch_shapes=[
            pltpu.VMEM((2, DIL_TILE, LANES), F32),
            pltpu.VMEM((2, DIL_TILE + 2 * halo, LANES), F32),
            pltpu.VMEM((2, DIL_TILE + 2 * halo, LANES), F32),
            pltpu.VMEM((2, DIL_TILE, LANES), F32),
            pltpu.VMEM((2, DIL_TILE, LANES), F32),
        ],
        compiler_params=_params(("parallel", "parallel")),
        name=f"dil_attn_d{dil}",
    )(zd, zd, zd, zd, zd, zd, zd)


def _out_kernel(x_ref, mod_ref, g1_ref, wg_ref, oa_ref, o0_ref, o1_ref, o2_ref, l0_ref, l1_ref, l2_ref,
                pa_ref, pb_ref, wo_ref, x1_ref):
    x = x_ref[0]
    mod = mod_ref[0]
    h = _rms(x, g1_ref[...]) * (1.0 + mod[1:2]) + mod[0:1]
    hb = h.astype(BF16)
    l0, l1, l2 = l0_ref[0], l1_ref[0], l2_ref[0]
    lm = jnp.maximum(jnp.maximum(l0, l1), l2)
    e0, e1, e2 = jnp.exp(l0 - lm), jnp.exp(l1 - lm), jnp.exp(l2 - lm)
    ob = (e0 * o0_ref[0] + e1 * o1_ref[0] + e2 * o2_ref[0]) / (e0 + e1 + e2)
    merged = jax.nn.sigmoid(_dot(hb, wg_ref[:, :D_MODEL])) * _dot(oa_ref[0], pa_ref[...])
    merged = merged + jax.nn.sigmoid(_dot(hb, wg_ref[:, D_MODEL:])) * _dot(ob.astype(BF16), pb_ref[...])
    x1_ref[0] = x + mod[2:3] * _dot(merged.astype(BF16), wo_ref[...])


def _out_call(x, mod, w, oa, dil_outs):
    B, S, _ = x.shape
    T = ROW_TILE
    row = lambda c: pl.BlockSpec((1, T, c), lambda b, i: (b, i, 0))
    os_ = [o for o, _ in dil_outs]
    ls_ = [l for _, l in dil_outs]
    return pl.pallas_call(
        _out_kernel,
        grid=(B, S // T),
        in_specs=[
            row(D_MODEL),
            pl.BlockSpec((1, 8, D_MODEL), lambda b, i: (b, 0, 0)),
            _const_spec((1, D_MODEL)),
            _const_spec((D_MODEL, 2 * D_MODEL)),
            row(MLA_HEADS * V_DIM),
            row(DIL_W), row(DIL_W), row(DIL_W), row(DIL_W), row(DIL_W), row(DIL_W),
            _const_spec((MLA_HEADS * V_DIM, D_MODEL)),
            _const_spec((DIL_W, D_MODEL)),
            _const_spec((D_MODEL, D_MODEL)),
        ],
        out_specs=row(D_MODEL),
        out_shape=jax.ShapeDtypeStruct((B, S, D_MODEL), F32),
        compiler_params=_params(("parallel", "parallel")),
        name="out_proj",
    )(x, mod, w["g1"], w["wg"], oa, *os_, *ls_, w["pa"], w["pb"], w["wo"])


def _ffn_kernel(xc_ref, xp_ref, xn_ref, mod_ref, g2_ref, wu_ref, cw_ref, cb_ref, wdn_ref, gf_ref, y_ref,
                h2s, a_s):
    T = xc_ref.shape[1]
    H = FFN_HALO
    mod = mod_ref[0]
    g2 = g2_ref[...]
    i = pl.program_id(1)
    n = pl.num_programs(1)

    def modnorm(v):
        return _rms(v, g2) * (1.0 + mod[4:5]) + mod[3:4]

    x1 = xc_ref[0]
    h2s[0:H] = jnp.where(i > 0, modnorm(xp_ref[0]), 0.0).astype(BF16)
    h2s[H:H + T] = modnorm(x1).astype(BF16)
    h2s[H + T:] = jnp.where(i < n - 1, modnorm(xn_ref[0]), 0.0).astype(BF16)

    rows = T + 2 * H
    for c in range(D_FF // FF_CHUNK):
        cs = slice(c * FF_CHUNK, (c + 1) * FF_CHUNK)
        ul = _dot(h2s[...], wu_ref[:, cs])
        gv = _dot(h2s[H:H + T], wu_ref[:, D_FF + c * FF_CHUNK:D_FF + (c + 1) * FF_CHUNK])
        u = (cb_ref[:, cs]
             + pltpu.roll(ul, 1, 0)[H:H + T] * cw_ref[0:1, cs]
             + ul[H:H + T] * cw_ref[1:2, cs]
             + pltpu.roll(ul, rows - 1, 0)[H:H + T] * cw_ref[2:3, cs])
        a_s[:, cs] = (jax.nn.gelu(u, approximate=True) * gv).astype(BF16)

    x2 = x1 + mod[5:6] * _dot(a_s[...], wdn_ref[...])
    y_ref[0] = _rms(x2, gf_ref[...])


def _ffn_call(x1, mod, w):
    B, S, _ = x1.shape
    T = ROW_TILE
    H = FFN_HALO
    per = T // H
    n_h = S // H
    return pl.pallas_call(
        _ffn_kernel,
        grid=(B, S // T),
        in_specs=[
            pl.BlockSpec((1, T, D_MODEL), lambda b, i: (b, i, 0)),
            pl.BlockSpec((1, H, D_MODEL), lambda b, i: (b, jnp.maximum(i * per - 1, 0), 0)),
            pl.BlockSpec((1, H, D_MODEL), lambda b, i: (b, jnp.minimum((i + 1) * per, n_h - 1), 0)),
            pl.BlockSpec((1, 8, D_MODEL), lambda b, i: (b, 0, 0)),
            _const_spec((1, D_MODEL)),
            _const_spec((D_MODEL, 2 * D_FF)),
            _const_spec((3, D_FF)),
            _const_spec((1, D_FF)),
            _const_spec((D_FF, D_MODEL)),
            _const_spec((1, D_MODEL)),
        ],
        out_specs=pl.BlockSpec((1, T, D_MODEL), lambda b, i: (b, i, 0)),
        out_shape=jax.ShapeDtypeStruct((B, S, D_MODEL), F32),
        scratch_shapes=[pltpu.VMEM((T + 2 * H, D_MODEL), BF16), pltpu.VMEM((T, D_FF), BF16)],
        compiler_params=_params(("parallel", "parallel")),
        name="conv_ffn",
    )(x1, x1, x1, mod, w["g2"], w["wu"], w["cw"], w["cb"], w["wdn"], w["gf"])


def _rope_tables(seq_len):
    inv = ROPE_THETA ** (-jnp.arange(0, QK_ROPE, 2, dtype=F32) / QK_ROPE)
    ang = jnp.arange(seq_len, dtype=F32)[:, None] * inv[None, :]
    cos, sin = jnp.cos(ang), jnp.sin(ang)
    zeros_l = jnp.zeros((seq_len, QK_NOPE), F32)
    zeros_r = jnp.zeros((seq_len, LANES - QK_NOPE - QK_ROPE), F32)
    ones_l = jnp.ones((seq_len, QK_NOPE), F32)
    scale = (QK_NOPE + QK_ROPE) ** -0.5
    t1k = jnp.concatenate([zeros_l, cos, cos, zeros_r], axis=1)
    t2 = jnp.concatenate([zeros_l, -sin, sin, zeros_r], axis=1)
    t1q = jnp.concatenate([ones_l, cos, cos, zeros_r], axis=1) * scale
    return t1q, t2 * scale, t1k, t2


def _prep_weights(norm1_g, w_in, q_norm_g, kv_norm_g, w_uq, w_ukv, p_a, p_b, w_out, norm2_g, w_up,
                  conv_w, conv_b, w_down, normf_g):
    half = QK_ROPE // 2
    c0 = Q_LORA + KV_LORA
    kr = w_in[:, c0:c0 + QK_ROPE]
    kr_chunk = jnp.concatenate(
        [jnp.zeros((D_MODEL, QK_NOPE), F32), kr, kr[:, half:], kr[:, :half]], axis=1)
    d0 = c0 + QK_ROPE
    wa = jnp.concatenate([w_in[:, :c0], kr_chunk], axis=1)
    wd = w_in[:, d0:d0 + ZD_COLS]
    wg = w_in[:, d0 + ZD_COLS:]
    uq = w_uq.reshape(Q_LORA, MLA_HEADS, QK_NOPE + QK_ROPE)
    rope = uq[:, :, QK_NOPE:]
    wq = jnp.concatenate([uq, rope[:, :, half:], rope[:, :, :half]], axis=2).reshape(Q_LORA, -1)
    ukv = w_ukv.reshape(KV_LORA, MLA_HEADS, QK_NOPE + V_DIM)
    wk = jnp.concatenate([ukv[:, :, :QK_NOPE], jnp.zeros((KV_LORA, MLA_HEADS, HEAD_PAD - QK_NOPE), F32)],
                         axis=2).reshape(KV_LORA, -1)
    wv = ukv[:, :, QK_NOPE:].reshape(KV_LORA, -1)
    b = lambda a: a.astype(BF16)
    return dict(
        g1=norm1_g.reshape(1, -1), wa=b(wa), wd=b(wd), wg=b(wg),
        qg=q_norm_g.reshape(1, -1), kvg=kv_norm_g.reshape(1, -1),
        wq=b(wq), wk=b(wk), wv=b(wv), pa=b(p_a), pb=b(p_b), wo=b(w_out),
        g2=norm2_g.reshape(1, -1), wu=b(w_up), cw=conv_w, cb=conv_b.reshape(1, -1), wdn=b(w_down),
        gf=normf_g.reshape(1, -1),
    )


def _trunk(x, mod, w):
    S = x.shape[1]
    assert S % DIL_TILE == 0 and S % ROW_TILE == 0 and S % MLA_TK == 0
    q, k, v, zd = _in_call(x, mod, w, _rope_tables(S))
    oa = _mla_call(q, k, v)
    dil_outs = [_dil_call(zd, g, dil) for g, (_, dil) in enumerate(DIL_CONFIGS)]
    x1 = _out_call(x, mod, w, oa, dil_outs)
    return _ffn_call(x1, mod, w)


def _mods(c_list, ada_w, ada_b):
    c_all = jnp.concatenate(c_list, axis=0)
    n = c_all.shape[0]
    rows = -(-n // 8) * 8
    c_all = jnp.pad(c_all, ((0, rows - n), (0, 0)))
    mod = _mod_call(c_all, ada_w, ada_b.reshape(1, -1)).reshape(rows, 6, D_MODEL)
    mod = jnp.pad(mod, ((0, 0), (0, 2), (0, 0)))
    out, o = [], 0
    for c in c_list:
        out.append(mod[o:o + c.shape[0]])
        o += c.shape[0]
    return out


def kernel(x_prompt, x_sample, c_prompt, c_sample, ada_w, ada_b, norm1_g, w_in, q_norm_g, kv_norm_g, w_uq, w_ukv, p_a, p_b, w_out, norm2_g, w_up, conv_w, conv_b, w_down, normf_g):
    assert ada_w.shape[0] == 1, "single layer"
    w = _prep_weights(norm1_g[0], w_in[0], q_norm_g[0], kv_norm_g[0], w_uq[0], w_ukv[0], p_a[0], p_b[0],
                      w_out[0], norm2_g[0], w_up[0], conv_w[0], conv_b[0], w_down[0], normf_g)
    mod_p, mod_s = _mods([c_prompt, c_sample], ada_w[0], ada_b[0])
    return (_trunk(x_prompt, mod_p, w), _trunk(x_sample, mod_s, w))
```

```python
import functools
import math

import jax
import jax.numpy as jnp
import numpy as np
from jax import lax
from jax.experimental import pallas as pl
from jax.experimental.pallas import tpu as pltpu

F32 = jnp.float32
BF16 = jnp.bfloat16

D_MODEL = 1024
MLA_HEADS = 16
QK_NOPE = 64
QK_ROPE = 32
V_DIM = 64
Q_LORA = 384
KV_LORA = 256
ROPE_THETA = 10000.0
DIL_CONFIGS = ((128, 1), (512, 4), (2048, 16))
DIL_HEADS_PER_GROUP = 4
DIL_HEADS = DIL_HEADS_PER_GROUP * len(DIL_CONFIGS)
DIL_HEAD_DIM = 64
D_FF = 2816
EPS = 1e-6

LANES = 128
HEAD_PAD = 128
DIL_W = DIL_HEADS_PER_GROUP * DIL_HEAD_DIM
N_SIDE = 64
Q_BLK = 128
K_WIN = Q_BLK + 2 * N_SIDE
DIL_TILE = 2048
ROW_TILE = 512
FFN_HALO = 16
FF_CHUNK = 256
MLA_TQ = 512
MLA_TK = 512
MLA_LROWS = 16
NEG = -1e30
VMEM_LIMIT = 56 * 1024 * 1024

ZA_COLS = Q_LORA + KV_LORA + LANES
ZD_COLS = 3 * DIL_HEADS * DIL_HEAD_DIM


def _params(sem):
    return pltpu.CompilerParams(dimension_semantics=sem, vmem_limit_bytes=VMEM_LIMIT)


def _const_spec(shape):
    nd = len(shape)
    return pl.BlockSpec(shape, lambda *_: (0,) * nd, pipeline_mode=pl.Buffered(1))


def _rms(x, g):
    return x * lax.rsqrt(jnp.mean(x * x, axis=-1, keepdims=True) + EPS) * g


def _dot(a, b):
    return jnp.dot(a, b, preferred_element_type=F32)


def _dot_nt(a, b):
    return lax.dot_general(a, b, (((1,), (1,)), ((), ())), preferred_element_type=F32)


def _mod_kernel(c_ref, w_ref, b_ref, o_ref):
    c = c_ref[...]
    a = c * jax.nn.sigmoid(c)
    a_hi = a.astype(BF16)
    a_lo = (a - a_hi.astype(F32)).astype(BF16)
    w = w_ref[...]
    w_hi = w.astype(BF16)
    w_lo = (w - w_hi.astype(F32)).astype(BF16)
    o_ref[...] = _dot(a_hi, w_hi) + _dot(a_lo, w_hi) + _dot(a_hi, w_lo) + b_ref[...]


def _mod_call(c_all, ada_w, ada_b):
    rows = c_all.shape[0]
    n = ada_w.shape[1]
    tn = 1536
    return pl.pallas_call(
        _mod_kernel,
        grid=(n // tn,),
        in_specs=[
            pl.BlockSpec((rows, D_MODEL), lambda j: (0, 0)),
            pl.BlockSpec((D_MODEL, tn), lambda j: (0, j)),
            pl.BlockSpec((1, tn), lambda j: (0, j)),
        ],
        out_specs=pl.BlockSpec((rows, tn), lambda j: (0, j)),
        out_shape=jax.ShapeDtypeStruct((rows, n), F32),
        compiler_params=_params(("arbitrary",)),
        name="mod",
    )(c_all, ada_w, ada_b)


def _in_kernel(x_ref, mod_ref, g1_ref, wa_ref, wd_ref, qg_ref, kvg_ref, wq_ref, wk_ref, wvt_ref,
               t1q_ref, t2q_ref, t1k_ref, t2k_ref, q_ref, k_ref, vt_ref, zd_ref):
    x = x_ref[0]
    mod = mod_ref[0]
    h = _rms(x, g1_ref[...]) * (1.0 + mod[1:2]) + mod[0:1]
    hb = h.astype(BF16)
    zd_ref[0] = _dot(hb, wd_ref[...]).astype(BF16)
    za = _dot(hb, wa_ref[...])
    cq = _rms(za[:, :Q_LORA], qg_ref[...]).astype(BF16)
    ckv = _rms(za[:, Q_LORA:Q_LORA + KV_LORA], kvg_ref[...]).astype(BF16)
    kr = za[:, Q_LORA + KV_LORA:]
    kr = kr * t1k_ref[...] + pltpu.roll(kr, LANES - QK_ROPE, 1) * t2k_ref[...]
    vt_ref[0] = _dot_nt(wvt_ref[...], ckv).astype(BF16)
    qf = _dot(cq, wq_ref[...])
    kf = _dot(ckv, wk_ref[...])
    t1q = t1q_ref[...]
    t2q = t2q_ref[...]
    for hd in range(MLA_HEADS):
        sl = slice(hd * HEAD_PAD, (hd + 1) * HEAD_PAD)
        qh = qf[:, sl]
        q_ref[0, :, sl] = (qh * t1q + pltpu.roll(qh, LANES - QK_ROPE, 1) * t2q).astype(BF16)
        k_ref[0, :, sl] = (kf[:, sl] + kr).astype(BF16)


def _in_call(x, mod, w, tabs):
    B, S, _ = x.shape
    T = ROW_TILE
    row = lambda c: pl.BlockSpec((1, T, c), lambda b, i: (b, i, 0))
    tab = pl.BlockSpec((T, LANES), lambda b, i: (i, 0))
    qk_cols = MLA_HEADS * HEAD_PAD
    return pl.pallas_call(
        _in_kernel,
        grid=(B, S // T),
        in_specs=[
            row(D_MODEL),
            pl.BlockSpec((1, 8, D_MODEL), lambda b, i: (b, 0, 0)),
            _const_spec((1, D_MODEL)),
            _const_spec((D_MODEL, ZA_COLS)),
            _const_spec((D_MODEL, ZD_COLS)),
            _const_spec((1, Q_LORA)),
            _const_spec((1, KV_LORA)),
            _const_spec((Q_LORA, qk_cols)),
            _const_spec((KV_LORA, qk_cols)),
            _const_spec((MLA_HEADS * V_DIM, KV_LORA)),
            tab, tab, tab, tab,
        ],
        out_specs=[row(qk_cols), row(qk_cols),
                   pl.BlockSpec((1, MLA_HEADS * V_DIM, T), lambda b, i: (b, 0, i)), row(ZD_COLS)],
        out_shape=[
            jax.ShapeDtypeStruct((B, S, qk_cols), BF16),
            jax.ShapeDtypeStruct((B, S, qk_cols), BF16),
            jax.ShapeDtypeStruct((B, MLA_HEADS * V_DIM, S), BF16),
            jax.ShapeDtypeStruct((B, S, ZD_COLS), BF16),
        ],
        compiler_params=_params(("parallel", "parallel")),
        name="in_proj",
    )(x, mod, w["g1"], w["wa"], w["wd"], w["qg"], w["kvg"], w["wq"], w["wk"], w["wvt"], *tabs)


def _mla_kernel(q_ref, k_ref, vt_ref, o_ref, s0_ref, s1_ref, acc_ref, *, n_kv):
    tq = q_ref.shape[1]
    ones = jnp.ones((MLA_LROWS, MLA_TK), BF16)
    slots = (s0_ref, s1_ref)

    def kv_rows(j):
        return pl.ds(pl.multiple_of(j * MLA_TK, MLA_TK), MLA_TK)

    def scores(j, slot):
        mx = []
        for hh in range(2):
            sl = slice(hh * HEAD_PAD, (hh + 1) * HEAD_PAD)
            st = _dot_nt(k_ref[0, kv_rows(j), sl], q_ref[0, :, sl])
            slots[slot][hh] = st
            mx.append(jnp.max(st, axis=0, keepdims=True))
        return tuple(mx)

    def accumulate(j, slot, m, mx):
        out = []
        for hh in range(2):
            m_new = jnp.maximum(m[hh], mx[hh])
            alpha = jnp.exp2(m[hh] - m_new)
            pt = jnp.exp2(slots[slot][hh] - m_new).astype(BF16)
            lhs = jnp.concatenate([vt_ref[0, hh * V_DIM:(hh + 1) * V_DIM, kv_rows(j)], ones], axis=0)
            acc_ref[hh] = alpha * acc_ref[hh] + _dot(lhs, pt)
            out.append(m_new)
        return tuple(out)

    acc_ref[...] = jnp.zeros_like(acc_ref)
    m = tuple(jnp.full((1, tq), -jnp.inf, F32) for _ in range(2))
    mx = scores(0, 0)

    def body(i, carry):
        m, mx = carry
        j = 2 * i
        mx1 = scores(j + 1, 1)
        m = accumulate(j, 0, m, mx)
        mx = scores(j + 2, 0)
        m = accumulate(j + 1, 1, m, mx1)
        return m, mx

    m, mx = lax.fori_loop(0, n_kv // 2 - 1, body, (m, mx))
    mx1 = scores(n_kv - 1, 1)
    m = accumulate(n_kv - 2, 0, m, mx)
    accumulate(n_kv - 1, 1, m, mx1)
    ot = jnp.concatenate([acc_ref[hh, :V_DIM] / acc_ref[hh, V_DIM:V_DIM + 1] for hh in range(2)], axis=0)
    o_ref[0] = ot.T.astype(BF16)


def _mla_call(q, k, vt):
    B, S, _ = q.shape
    tq = MLA_TQ
    return pl.pallas_call(
        functools.partial(_mla_kernel, n_kv=S // MLA_TK),
        grid=(B, MLA_HEADS // 2, S // tq),
        in_specs=[
            pl.BlockSpec((1, tq, 2 * HEAD_PAD), lambda b, h, i: (b, i, h)),
            pl.BlockSpec((1, S, 2 * HEAD_PAD), lambda b, h, i: (b, 0, h)),
            pl.BlockSpec((1, 2 * V_DIM, S), lambda b, h, i: (b, h, 0)),
        ],
        out_specs=pl.BlockSpec((1, tq, 2 * V_DIM), lambda b, h, i: (b, i, h)),
        out_shape=jax.ShapeDtypeStruct((B, S, MLA_HEADS * V_DIM), BF16),
        scratch_shapes=[
            pltpu.VMEM((2, MLA_TK, tq), F32),
            pltpu.VMEM((2, MLA_TK, tq), F32),
            pltpu.VMEM((2, V_DIM + MLA_LROWS, tq), F32),
        ],
        compiler_params=_params(("parallel", "parallel", "arbitrary")),
        name="mla_attn",
    )(q, k, vt)


def _dil_kernel(q_ref, kp_ref, kc_ref, kn_ref, vp_ref, vc_ref, vn_ref, o_ref, lse_ref,
                qs, ks, vs, os_, ls_, *, dil, group, seq_len):
    halo = N_SIDE * dil
    n_blk = DIL_TILE // (Q_BLK * dil)
    sc_len = seq_len // dil
    i0 = pl.program_id(1) * (DIL_TILE // dil)

    for pair in range(2):
        cs = slice(pair * LANES, (pair + 1) * LANES)
        qs[pair] = q_ref[0, :, cs].astype(F32)
        for dst, (p_ref, c_ref, n_ref) in ((ks, (kp_ref, kc_ref, kn_ref)), (vs, (vp_ref, vc_ref, vn_ref))):
            dst[pair, 0:halo] = p_ref[0, :, cs].astype(F32)
            dst[pair, halo:halo + DIL_TILE] = c_ref[0, :, cs].astype(F32)
            dst[pair, halo + DIL_TILE:] = n_ref[0, :, cs].astype(F32)

    col = lax.broadcasted_iota(jnp.int32, (Q_BLK, K_WIN), 1)
    rowi = lax.broadcasted_iota(jnp.int32, (Q_BLK, K_WIN), 0)
    rel = col - N_SIDE - rowi
    absrel = jnp.abs(rel).astype(F32)
    band = jnp.abs(rel) <= N_SIDE
    lane = lax.broadcasted_iota(jnp.int32, (Q_BLK, LANES), 1)
    first_head = lane < DIL_HEAD_DIM
    scale = DIL_HEAD_DIM ** -0.5

    for jb in range(n_blk):
        kc0 = i0 + (jb * Q_BLK - N_SIDE)
        valid = band & (col >= -kc0) & (col < sc_len - kc0)
        for r in range(dil):
            start = jb * Q_BLK * dil + r
            q_rows = pl.ds(start, Q_BLK, stride=dil) if dil > 1 else pl.ds(start, Q_BLK)
            k_rows = pl.ds(start, K_WIN, stride=dil) if dil > 1 else pl.ds(start, K_WIN)
            for pair in range(2):
                qp = qs[pair, q_rows, :] * scale
                kp = ks[pair, k_rows, :].astype(BF16)
                vp = vs[pair, k_rows, :].astype(BF16)
                o_acc = None
                lse_acc = None
                for hh in range(2):
                    head = group * DIL_HEADS_PER_GROUP + pair * 2 + hh
                    slope = 2.0 ** (-8.0 * (head + 1) / DIL_HEADS)
                    sel = first_head if hh == 0 else jnp.logical_not(first_head)
                    qm = jnp.where(sel, qp, 0.0).astype(BF16)
                    s = _dot_nt(qm, kp) - absrel * (slope * dil)
                    s = jnp.where(valid, s, NEG)
                    m = jnp.max(s, axis=1, keepdims=True)
                    p = jnp.exp(s - m)
                    l = jnp.sum(p, axis=1, keepdims=True)
                    o = _dot(p.astype(BF16), vp) / l
                    lse = jnp.broadcast_to(m + jnp.log(l), (Q_BLK, LANES))
                    o_acc = o if hh == 0 else jnp.where(sel, o, o_acc)
                    lse_acc = lse if hh == 0 else jnp.where(sel, lse, lse_acc)
                os_[pair, q_rows, :] = o_acc
                ls_[pair, q_rows, :] = lse_acc

    for pair in range(2):
        cs = slice(pair * LANES, (pair + 1) * LANES)
        o_ref[0, :, cs] = os_[pair]
        lse_ref[0, :, cs] = ls_[pair]


def _dil_call(zd, group, dil):
    B, S, _ = zd.shape
    halo = N_SIDE * dil
    per = DIL_TILE // halo
    n_halo = S // halo
    g = group
    cur = lambda c: pl.BlockSpec((1, DIL_TILE, DIL_W), lambda b, i: (b, i, c))
    prev = lambda c: pl.BlockSpec((1, halo, DIL_W), lambda b, i: (b, jnp.maximum(i * per - 1, 0), c))
    nxt = lambda c: pl.BlockSpec((1, halo, DIL_W), lambda b, i: (b, jnp.minimum((i + 1) * per, n_halo - 1), c))
    n_g = len(DIL_CONFIGS)
    out_spec = pl.BlockSpec((1, DIL_TILE, DIL_W), lambda b, i: (b, i, 0))
    return pl.pallas_call(
        functools.partial(_dil_kernel, dil=dil, group=group, seq_len=S),
        grid=(B, S // DIL_TILE),
        in_specs=[cur(g), prev(n_g + g), cur(n_g + g), nxt(n_g + g),
                  prev(2 * n_g + g), cur(2 * n_g + g), nxt(2 * n_g + g)],
        out_specs=[out_spec, out_spec],
        out_shape=[jax.ShapeDtypeStruct((B, S, DIL_W), F32), jax.ShapeDtypeStruct((B, S, DIL_W), F32)],
        scratch_shapes=[
            pltpu.VMEM((2, DIL_TILE, LANES), F32),
            pltpu.VMEM((2, DIL_TILE + 2 * halo, LANES), F32),
            pltpu.VMEM((2, DIL_TILE + 2 * halo, LANES), F32),
            pltpu.VMEM((2, DIL_TILE, LANES), F32),
            pltpu.VMEM((2, DIL_TILE, LANES), F32),
        ],
        compiler_params=_params(("parallel", "parallel")),
        name=f"dil_attn_d{dil}",
    )(zd, zd, zd, zd, zd, zd, zd)


def _out_kernel(x_ref, mod_ref, g1_ref, wg_ref, oa_ref, o0_ref, o1_ref, o2_ref, l0_ref, l1_ref, l2_ref,
                pa_ref, pb_ref, wo_ref, x1_ref):
    x = x_ref[0]
    mod = mod_ref[0]
    h = _rms(x, g1_ref[...]) * (1.0 + mod[1:2]) + mod[0:1]
    hb = h.astype(BF16)
    l0, l1, l2 = l0_ref[0], l1_ref[0], l2_ref[0]
    lm = jnp.maximum(jnp.maximum(l0, l1), l2)
    e0, e1, e2 = jnp.exp(l0 - lm), jnp.exp(l1 - lm), jnp.exp(l2 - lm)
    ob = (e0 * o0_ref[0] + e1 * o1_ref[0] + e2 * o2_ref[0]) / (e0 + e1 + e2)
    merged = jax.nn.sigmoid(_dot(hb, wg_ref[:, :D_MODEL])) * _dot(oa_ref[0], pa_ref[...])
    merged = merged + jax.nn.sigmoid(_dot(hb, wg_ref[:, D_MODEL:])) * _dot(ob.astype(BF16), pb_ref[...])
    x1_ref[0] = x + mod[2:3] * _dot(merged.astype(BF16), wo_ref[...])


def _out_call(x, mod, w, oa, dil_outs):
    B, S, _ = x.shape
    T = ROW_TILE
    row = lambda c: pl.BlockSpec((1, T, c), lambda b, i: (b, i, 0))
    os_ = [o for o, _ in dil_outs]
    ls_ = [l for _, l in dil_outs]
    return pl.pallas_call(
        _out_kernel,
        grid=(B, S // T),
        in_specs=[
            row(D_MODEL),
            pl.BlockSpec((1, 8, D_MODEL), lambda b, i: (b, 0, 0)),
            _const_spec((1, D_MODEL)),
            _const_spec((D_MODEL, 2 * D_MODEL)),
            row(MLA_HEADS * V_DIM),
            row(DIL_W), row(DIL_W), row(DIL_W), row(DIL_W), row(DIL_W), row(DIL_W),
            _const_spec((MLA_HEADS * V_DIM, D_MODEL)),
            _const_spec((DIL_W, D_MODEL)),
            _const_spec((D_MODEL, D_MODEL)),
        ],
        out_specs=row(D_MODEL),
        out_shape=jax.ShapeDtypeStruct((B, S, D_MODEL), F32),
        compiler_params=_params(("parallel", "parallel")),
        name="out_proj",
    )(x, mod, w["g1"], w["wg"], oa, *os_, *ls_, w["pa"], w["pb"], w["wo"])


def _ffn_kernel(xc_ref, xp_ref, xn_ref, mod_ref, g2_ref, wu_ref, cw_ref, cb_ref, wdn_ref, gf_ref, y_ref,
                h2s, a_s):
    T = xc_ref.shape[1]
    H = FFN_HALO
    mod = mod_ref[0]
    g2 = g2_ref[...]
    i = pl.program_id(1)
    n = pl.num_programs(1)

    def modnorm(v):
        return _rms(v, g2) * (1.0 + mod[4:5]) + mod[3:4]

    x1 = xc_ref[0]
    h2s[0:H] = jnp.where(i > 0, modnorm(xp_ref[0]), 0.0).astype(BF16)
    h2s[H:H + T] = modnorm(x1).astype(BF16)
    h2s[H + T:] = jnp.where(i < n - 1, modnorm(xn_ref[0]), 0.0).astype(BF16)

    rows = T + 2 * H
    for c in range(D_FF // FF_CHUNK):
        cs = slice(c * FF_CHUNK, (c + 1) * FF_CHUNK)
        ul = _dot(h2s[...], wu_ref[:, cs])
        gv = _dot(h2s[H:H + T], wu_ref[:, D_FF + c * FF_CHUNK:D_FF + (c + 1) * FF_CHUNK])
        u = (cb_ref[:, cs]
             + pltpu.roll(ul, 1, 0)[H:H + T] * cw_ref[0:1, cs]
             + ul[H:H + T] * cw_ref[1:2, cs]
             + pltpu.roll(ul, rows - 1, 0)[H:H + T] * cw_ref[2:3, cs])
        a_s[:, cs] = (jax.nn.gelu(u, approximate=True) * gv).astype(BF16)

    x2 = x1 + mod[5:6] * _dot(a_s[...], wdn_ref[...])
    y_ref[0] = _rms(x2, gf_ref[...])


def _ffn_call(x1, mod, w):
    B, S, _ = x1.shape
    T = ROW_TILE
    H = FFN_HALO
    per = T // H
    n_h = S // H
    return pl.pallas_call(
        _ffn_kernel,
        grid=(B, S // T),
        in_specs=[
            pl.BlockSpec((1, T, D_MODEL), lambda b, i: (b, i, 0)),
            pl.BlockSpec((1, H, D_MODEL), lambda b, i: (b, jnp.maximum(i * per - 1, 0), 0)),
            pl.BlockSpec((1, H, D_MODEL), lambda b, i: (b, jnp.minimum((i + 1) * per, n_h - 1), 0)),
            pl.BlockSpec((1, 8, D_MODEL), lambda b, i: (b, 0, 0)),
            _const_spec((1, D_MODEL)),
            _const_spec((D_MODEL, 2 * D_FF)),
            _const_spec((3, D_FF)),
            _const_spec((1, D_FF)),
            _const_spec((D_FF, D_MODEL)),
            _const_spec((1, D_MODEL)),
        ],
        out_specs=pl.BlockSpec((1, T, D_MODEL), lambda b, i: (b, i, 0)),
        out_shape=jax.ShapeDtypeStruct((B, S, D_MODEL), F32),
        scratch_shapes=[pltpu.VMEM((T + 2 * H, D_MODEL), BF16), pltpu.VMEM((T, D_FF), BF16)],
        compiler_params=_params(("parallel", "parallel")),
        name="conv_ffn",
    )(x1, x1, x1, mod, w["g2"], w["wu"], w["cw"], w["cb"], w["wdn"], w["gf"])


def _rope_tables(seq_len):
    inv = ROPE_THETA ** (-jnp.arange(0, QK_ROPE, 2, dtype=F32) / QK_ROPE)
    ang = jnp.arange(seq_len, dtype=F32)[:, None] * inv[None, :]
    cos, sin = jnp.cos(ang), jnp.sin(ang)
    zeros_l = jnp.zeros((seq_len, QK_NOPE), F32)
    zeros_r = jnp.zeros((seq_len, LANES - QK_NOPE - QK_ROPE), F32)
    ones_l = jnp.ones((seq_len, QK_NOPE), F32)
    scale = (QK_NOPE + QK_ROPE) ** -0.5 * math.log2(math.e)
    t1k = jnp.concatenate([zeros_l, cos, cos, zeros_r], axis=1)
    t2 = jnp.concatenate([zeros_l, -sin, sin, zeros_r], axis=1)
    t1q = jnp.concatenate([ones_l, cos, cos, zeros_r], axis=1) * scale
    return t1q, t2 * scale, t1k, t2


def _prep_weights(norm1_g, w_in, q_norm_g, kv_norm_g, w_uq, w_ukv, p_a, p_b, w_out, norm2_g, w_up,
                  conv_w, conv_b, w_down, normf_g):
    half = QK_ROPE // 2
    c0 = Q_LORA + KV_LORA
    kr = w_in[:, c0:c0 + QK_ROPE]
    kr_chunk = jnp.concatenate(
        [jnp.zeros((D_MODEL, QK_NOPE), F32), kr, kr[:, half:], kr[:, :half]], axis=1)
    d0 = c0 + QK_ROPE
    wa = jnp.concatenate([w_in[:, :c0], kr_chunk], axis=1)
    wd = w_in[:, d0:d0 + ZD_COLS]
    wg = w_in[:, d0 + ZD_COLS:]
    uq = w_uq.reshape(Q_LORA, MLA_HEADS, QK_NOPE + QK_ROPE)
    rope = uq[:, :, QK_NOPE:]
    wq = jnp.concatenate([uq, rope[:, :, half:], rope[:, :, :half]], axis=2).reshape(Q_LORA, -1)
    ukv = w_ukv.reshape(KV_LORA, MLA_HEADS, QK_NOPE + V_DIM)
    wk = jnp.concatenate([ukv[:, :, :QK_NOPE], jnp.zeros((KV_LORA, MLA_HEADS, HEAD_PAD - QK_NOPE), F32)],
                         axis=2).reshape(KV_LORA, -1)
    wv = ukv[:, :, QK_NOPE:].reshape(KV_LORA, -1)
    b = lambda a: a.astype(BF16)
    return dict(
        g1=norm1_g.reshape(1, -1), wa=b(wa), wd=b(wd), wg=b(wg),
        qg=q_norm_g.reshape(1, -1), kvg=kv_norm_g.reshape(1, -1),
        wq=b(wq), wk=b(wk), wvt=b(wv.T), pa=b(p_a), pb=b(p_b), wo=b(w_out),
        g2=norm2_g.reshape(1, -1), wu=b(w_up), cw=conv_w, cb=conv_b.reshape(1, -1), wdn=b(w_down),
        gf=normf_g.reshape(1, -1),
    )


def _trunk(x, mod, w):
    S = x.shape[1]
    assert S % DIL_TILE == 0 and S % ROW_TILE == 0 and S % (2 * MLA_TK) == 0 and S % MLA_TQ == 0
    q, k, vt, zd = _in_call(x, mod, w, _rope_tables(S))
    oa = _mla_call(q, k, vt)
    dil_outs = [_dil_call(zd, g, dil) for g, (_, dil) in enumerate(DIL_CONFIGS)]
    x1 = _out_call(x, mod, w, oa, dil_outs)
    return _ffn_call(x1, mod, w)


def _mods(c_list, ada_w, ada_b):
    c_all = jnp.concatenate(c_list, axis=0)
    n = c_all.shape[0]
    rows = -(-n // 8) * 8
    c_all = jnp.pad(c_all, ((0, rows - n), (0, 0)))
    mod = _mod_call(c_all, ada_w, ada_b.reshape(1, -1)).reshape(rows, 6, D_MODEL)
    mod = jnp.pad(mod, ((0, 0), (0, 2), (0, 0)))
    out, o = [], 0
    for c in c_list:
        out.append(mod[o:o + c.shape[0]])
        o += c.shape[0]
    return out


def kernel(x_prompt, x_sample, c_prompt, c_sample, ada_w, ada_b, norm1_g, w_in, q_norm_g, kv_norm_g, w_uq, w_ukv, p_a, p_b, w_out, norm2_g, w_up, conv_w, conv_b, w_down, normf_g):
    assert ada_w.shape[0] == 1, "single layer"
    w = _prep_weights(norm1_g[0], w_in[0], q_norm_g[0], kv_norm_g[0], w_uq[0], w_ukv[0], p_a[0], p_b[0],
                      w_out[0], norm2_g[0], w_up[0], conv_w[0], conv_b[0], w_down[0], normf_g)
    mod_p, mod_s = _mods([c_prompt, c_sample], ada_w[0], ada_b[0])
    return (_trunk(x_prompt, mod_p, w), _trunk(x_sample, mod_s, w))
```

```python
import functools
import math

import jax
import jax.numpy as jnp
import numpy as np
from jax import lax
from jax.experimental import pallas as pl
from jax.experimental.pallas import tpu as pltpu

F32 = jnp.float32
BF16 = jnp.bfloat16

D_MODEL = 1024
MLA_HEADS = 16
QK_NOPE = 64
QK_ROPE = 32
V_DIM = 64
Q_LORA = 384
KV_LORA = 256
ROPE_THETA = 10000.0
DIL_CONFIGS = ((128, 1), (512, 4), (2048, 16))
DIL_HEADS_PER_GROUP = 4
DIL_HEADS = DIL_HEADS_PER_GROUP * len(DIL_CONFIGS)
DIL_HEAD_DIM = 64
D_FF = 2816
EPS = 1e-6

LANES = 128
HEAD_PAD = 128
DIL_W = DIL_HEADS_PER_GROUP * DIL_HEAD_DIM
N_SIDE = 64
Q_BLK = 128
K_WIN = Q_BLK + 2 * N_SIDE
DIL_TILE = 2048
ROW_TILE = 512
FFN_HALO = 16
FF_CHUNK = 256
MLA_TQ = 512
MLA_TK = 512
MLA_UNROLL = 4
MLA_LROWS = 16
NEG = -1e30
VMEM_LIMIT = 56 * 1024 * 1024

ZA_COLS = Q_LORA + KV_LORA + LANES
ZD_COLS = 3 * DIL_HEADS * DIL_HEAD_DIM


def _params(sem):
    return pltpu.CompilerParams(dimension_semantics=sem, vmem_limit_bytes=VMEM_LIMIT)


def _const_spec(shape):
    nd = len(shape)
    return pl.BlockSpec(shape, lambda *_: (0,) * nd, pipeline_mode=pl.Buffered(1))


def _rms(x, g):
    return x * lax.rsqrt(jnp.mean(x * x, axis=-1, keepdims=True) + EPS) * g


def _dot(a, b):
    return jnp.dot(a, b, preferred_element_type=F32)


def _dot_nt(a, b):
    return lax.dot_general(a, b, (((1,), (1,)), ((), ())), preferred_element_type=F32)


def _mod_kernel(c_ref, w_ref, b_ref, o_ref):
    c = c_ref[...]
    a = c * jax.nn.sigmoid(c)
    a_hi = a.astype(BF16)
    a_lo = (a - a_hi.astype(F32)).astype(BF16)
    w = w_ref[...]
    w_hi = w.astype(BF16)
    w_lo = (w - w_hi.astype(F32)).astype(BF16)
    o_ref[...] = _dot(a_hi, w_hi) + _dot(a_lo, w_hi) + _dot(a_hi, w_lo) + b_ref[...]


def _mod_call(c_all, ada_w, ada_b):
    rows = c_all.shape[0]
    n = ada_w.shape[1]
    tn = 1536
    return pl.pallas_call(
        _mod_kernel,
        grid=(n // tn,),
        in_specs=[
            pl.BlockSpec((rows, D_MODEL), lambda j: (0, 0)),
            pl.BlockSpec((D_MODEL, tn), lambda j: (0, j)),
            pl.BlockSpec((1, tn), lambda j: (0, j)),
        ],
        out_specs=pl.BlockSpec((rows, tn), lambda j: (0, j)),
        out_shape=jax.ShapeDtypeStruct((rows, n), F32),
        compiler_params=_params(("arbitrary",)),
        name="mod",
    )(c_all, ada_w, ada_b)


def _in_kernel(x_ref, mod_ref, g1_ref, wa_ref, wd_ref, qg_ref, kvg_ref, wq_ref, wk_ref, wvt_ref,
               t1q_ref, t2q_ref, t1k_ref, t2k_ref, q_ref, k_ref, vt_ref, zd_ref):
    x = x_ref[0]
    mod = mod_ref[0]
    h = _rms(x, g1_ref[...]) * (1.0 + mod[1:2]) + mod[0:1]
    hb = h.astype(BF16)
    zd_ref[0] = _dot(hb, wd_ref[...]).astype(BF16)
    za = _dot(hb, wa_ref[...])
    cq = _rms(za[:, :Q_LORA], qg_ref[...]).astype(BF16)
    ckv = _rms(za[:, Q_LORA:Q_LORA + KV_LORA], kvg_ref[...]).astype(BF16)
    kr = za[:, Q_LORA + KV_LORA:]
    kr = kr * t1k_ref[...] + pltpu.roll(kr, LANES - QK_ROPE, 1) * t2k_ref[...]
    vt_ref[0] = _dot_nt(wvt_ref[...], ckv).astype(BF16)
    qf = _dot(cq, wq_ref[...])
    kf = _dot(ckv, wk_ref[...])
    t1q = t1q_ref[...]
    t2q = t2q_ref[...]
    for hd in range(MLA_HEADS):
        sl = slice(hd * HEAD_PAD, (hd + 1) * HEAD_PAD)
        qh = qf[:, sl]
        q_ref[0, :, sl] = (qh * t1q + pltpu.roll(qh, LANES - QK_ROPE, 1) * t2q).astype(BF16)
        k_ref[0, :, sl] = (kf[:, sl] + kr).astype(BF16)


def _in_call(x, mod, w, tabs):
    B, S, _ = x.shape
    T = ROW_TILE
    row = lambda c: pl.BlockSpec((1, T, c), lambda b, i: (b, i, 0))
    tab = pl.BlockSpec((T, LANES), lambda b, i: (i, 0))
    qk_cols = MLA_HEADS * HEAD_PAD
    return pl.pallas_call(
        _in_kernel,
        grid=(B, S // T),
        in_specs=[
            row(D_MODEL),
            pl.BlockSpec((1, 8, D_MODEL), lambda b, i: (b, 0, 0)),
            _const_spec((1, D_MODEL)),
            _const_spec((D_MODEL, ZA_COLS)),
            _const_spec((D_MODEL, ZD_COLS)),
            _const_spec((1, Q_LORA)),
            _const_spec((1, KV_LORA)),
            _const_spec((Q_LORA, qk_cols)),
            _const_spec((KV_LORA, qk_cols)),
            _const_spec((MLA_HEADS * V_DIM, KV_LORA)),
            tab, tab, tab, tab,
        ],
        out_specs=[row(qk_cols), row(qk_cols),
                   pl.BlockSpec((1, MLA_HEADS * V_DIM, T), lambda b, i: (b, 0, i)), row(ZD_COLS)],
        out_shape=[
            jax.ShapeDtypeStruct((B, S, qk_cols), BF16),
            jax.ShapeDtypeStruct((B, S, qk_cols), BF16),
            jax.ShapeDtypeStruct((B, MLA_HEADS * V_DIM, S), BF16),
            jax.ShapeDtypeStruct((B, S, ZD_COLS), BF16),
        ],
        compiler_params=_params(("parallel", "parallel")),
        name="in_proj",
    )(x, mod, w["g1"], w["wa"], w["wd"], w["qg"], w["kvg"], w["wq"], w["wk"], w["wvt"], *tabs)


def _mla_kernel(q_ref, k_ref, vt_ref, o_ref, s0_ref, s1_ref, acc_ref, *, n_q, n_kv):
    tq, tk = MLA_TQ, MLA_TK
    ones = jnp.ones((MLA_LROWS, tk), BF16)
    slots = (s0_ref, s1_ref)
    n_chunks = n_q * n_kv

    def q_rows(c):
        return pl.ds(pl.multiple_of((c // n_kv) * tq, tq), tq)

    def kv_rows(c):
        return pl.ds(pl.multiple_of((c % n_kv) * tk, tk), tk)

    def scores(c, slot):
        mx = []
        for hh in range(2):
            sl = slice(hh * HEAD_PAD, (hh + 1) * HEAD_PAD)
            st = _dot_nt(k_ref[0, kv_rows(c), sl], q_ref[0, q_rows(c), sl])
            slots[slot][hh] = st
            mx.append(jnp.max(st, axis=0, keepdims=True))
        return tuple(mx)

    def accumulate(c, slot, m, mx):
        first = (c % n_kv) == 0
        out = []
        for hh in range(2):
            m_old = jnp.where(first, -jnp.inf, m[hh])
            m_new = jnp.maximum(m_old, mx[hh])
            alpha = jnp.exp2(m_old - m_new)
            pt = jnp.exp2(slots[slot][hh] - m_new).astype(BF16)
            lhs = jnp.concatenate([vt_ref[0, hh * V_DIM:(hh + 1) * V_DIM, kv_rows(c)], ones], axis=0)
            acc_ref[hh] = alpha * acc_ref[hh] + _dot(lhs, pt)
            out.append(m_new)
        return tuple(out)

    def finalize(c):
        ot = jnp.concatenate([acc_ref[hh, :V_DIM] / acc_ref[hh, V_DIM:V_DIM + 1] for hh in range(2)], axis=0)
        o_ref[0, q_rows(c), :] = ot.T.astype(BF16)

    def run(c0, mx, m, prefetch_last):
        for t in range(MLA_UNROLL):
            if t < MLA_UNROLL - 1 or prefetch_last:
                mx_next = scores(c0 + t + 1, (t + 1) % 2)
            m = accumulate(c0 + t, t % 2, m, mx)
            mx = mx_next
        return m, mx

    acc_ref[...] = jnp.zeros_like(acc_ref)
    m = tuple(jnp.full((1, tq), -jnp.inf, F32) for _ in range(2))
    mx = scores(0, 0)

    def body(i, carry):
        c0 = MLA_UNROLL * i
        m, mx = run(c0, carry[1], carry[0], True)
        last = c0 + MLA_UNROLL - 1

        @pl.when(last % n_kv == n_kv - 1)
        def _():
            finalize(last)

        return m, mx

    m, mx = lax.fori_loop(0, n_chunks // MLA_UNROLL - 1, body, (m, mx))
    run(n_chunks - MLA_UNROLL, mx, m, False)
    finalize(n_chunks - 1)


def _mla_call(q, k, vt):
    B, S, _ = q.shape
    tq, tk = MLA_TQ, MLA_TK
    return pl.pallas_call(
        functools.partial(_mla_kernel, n_q=S // tq, n_kv=S // tk),
        grid=(B, MLA_HEADS // 2),
        in_specs=[
            pl.BlockSpec((1, S, 2 * HEAD_PAD), lambda b, h: (b, 0, h)),
            pl.BlockSpec((1, S, 2 * HEAD_PAD), lambda b, h: (b, 0, h)),
            pl.BlockSpec((1, 2 * V_DIM, S), lambda b, h: (b, h, 0)),
        ],
        out_specs=pl.BlockSpec((1, S, 2 * V_DIM), lambda b, h: (b, 0, h)),
        out_shape=jax.ShapeDtypeStruct((B, S, MLA_HEADS * V_DIM), BF16),
        scratch_shapes=[
            pltpu.VMEM((2, tk, tq), F32),
            pltpu.VMEM((2, tk, tq), F32),
            pltpu.VMEM((2, V_DIM + MLA_LROWS, tq), F32),
        ],
        compiler_params=_params(("parallel", "parallel")),
        name="mla_attn",
    )(q, k, vt)


def _dil_kernel(q_ref, kp_ref, kc_ref, kn_ref, vp_ref, vc_ref, vn_ref, o_ref, lse_ref,
                qs, ks, vs, os_, ls_, *, dil, group, seq_len):
    halo = N_SIDE * dil
    n_blk = DIL_TILE // (Q_BLK * dil)
    sc_len = seq_len // dil
    i0 = pl.program_id(1) * (DIL_TILE // dil)

    for pair in range(2):
        cs = slice(pair * LANES, (pair + 1) * LANES)
        qs[pair] = q_ref[0, :, cs].astype(F32)
        for dst, (p_ref, c_ref, n_ref) in ((ks, (kp_ref, kc_ref, kn_ref)), (vs, (vp_ref, vc_ref, vn_ref))):
            dst[pair, 0:halo] = p_ref[0, :, cs].astype(F32)
            dst[pair, halo:halo + DIL_TILE] = c_ref[0, :, cs].astype(F32)
            dst[pair, halo + DIL_TILE:] = n_ref[0, :, cs].astype(F32)

    col = lax.broadcasted_iota(jnp.int32, (Q_BLK, K_WIN), 1)
    rowi = lax.broadcasted_iota(jnp.int32, (Q_BLK, K_WIN), 0)
    rel = col - N_SIDE - rowi
    absrel = jnp.abs(rel).astype(F32)
    band = jnp.abs(rel) <= N_SIDE
    lane = lax.broadcasted_iota(jnp.int32, (Q_BLK, LANES), 1)
    first_head = lane < DIL_HEAD_DIM
    scale = DIL_HEAD_DIM ** -0.5

    for jb in range(n_blk):
        kc0 = i0 + (jb * Q_BLK - N_SIDE)
        valid = band & (col >= -kc0) & (col < sc_len - kc0)
        for r in range(dil):
            start = jb * Q_BLK * dil + r
            q_rows = pl.ds(start, Q_BLK, stride=dil) if dil > 1 else pl.ds(start, Q_BLK)
            k_rows = pl.ds(start, K_WIN, stride=dil) if dil > 1 else pl.ds(start, K_WIN)
            for pair in range(2):
                qp = qs[pair, q_rows, :] * scale
                kp = ks[pair, k_rows, :].astype(BF16)
                vp = vs[pair, k_rows, :].astype(BF16)
                o_acc = None
                lse_acc = None
                for hh in range(2):
                    head = group * DIL_HEADS_PER_GROUP + pair * 2 + hh
                    slope = 2.0 ** (-8.0 * (head + 1) / DIL_HEADS)
                    sel = first_head if hh == 0 else jnp.logical_not(first_head)
                    qm = jnp.where(sel, qp, 0.0).astype(BF16)
                    s = _dot_nt(qm, kp) - absrel * (slope * dil)
                    s = jnp.where(valid, s, NEG)
                    m = jnp.max(s, axis=1, keepdims=True)
                    p = jnp.exp(s - m)
                    l = jnp.sum(p, axis=1, keepdims=True)
                    o = _dot(p.astype(BF16), vp) / l
                    lse = jnp.broadcast_to(m + jnp.log(l), (Q_BLK, LANES))
                    o_acc = o if hh == 0 else jnp.where(sel, o, o_acc)
                    lse_acc = lse if hh == 0 else jnp.where(sel, lse, lse_acc)
                os_[pair, q_rows, :] = o_acc
                ls_[pair, q_rows, :] = lse_acc

    for pair in range(2):
        cs = slice(pair * LANES, (pair + 1) * LANES)
        o_ref[0, :, cs] = os_[pair]
        lse_ref[0, :, cs] = ls_[pair]


def _dil_call(zd, group, dil):
    B, S, _ = zd.shape
    halo = N_SIDE * dil
    per = DIL_TILE // halo
    n_halo = S // halo
    g = group
    cur = lambda c: pl.BlockSpec((1, DIL_TILE, DIL_W), lambda b, i: (b, i, c))
    prev = lambda c: pl.BlockSpec((1, halo, DIL_W), lambda b, i: (b, jnp.maximum(i * per - 1, 0), c))
    nxt = lambda c: pl.BlockSpec((1, halo, DIL_W), lambda b, i: (b, jnp.minimum((i + 1) * per, n_halo - 1), c))
    n_g = len(DIL_CONFIGS)
    out_spec = pl.BlockSpec((1, DIL_TILE, DIL_W), lambda b, i: (b, i, 0))
    return pl.pallas_call(
        functools.partial(_dil_kernel, dil=dil, group=group, seq_len=S),
        grid=(B, S // DIL_TILE),
        in_specs=[cur(g), prev(n_g + g), cur(n_g + g), nxt(n_g + g),
                  prev(2 * n_g + g), cur(2 * n_g + g), nxt(2 * n_g + g)],
        out_specs=[out_spec, out_spec],
        out_shape=[jax.ShapeDtypeStruct((B, S, DIL_W), F32), jax.ShapeDtypeStruct((B, S, DIL_W), F32)],
        scratch_shapes=[
            pltpu.VMEM((2, DIL_TILE, LANES), F32),
            pltpu.VMEM((2, DIL_TILE + 2 * halo, LANES), F32),
            pltpu.VMEM((2, DIL_TILE + 2 * halo, LANES), F32),
            pltpu.VMEM((2, DIL_TILE, LANES), F32),
            pltpu.VMEM((2, DIL_TILE, LANES), F32),
        ],
        compiler_params=_params(("parallel", "parallel")),
        name=f"dil_attn_d{dil}",
    )(zd, zd, zd, zd, zd, zd, zd)


def _out_kernel(x_ref, mod_ref, g1_ref, wg_ref, oa_ref, o0_ref, o1_ref, o2_ref, l0_ref, l1_ref, l2_ref,
                pa_ref, pb_ref, wo_ref, x1_ref):
    x = x_ref[0]
    mod = mod_ref[0]
    h = _rms(x, g1_ref[...]) * (1.0 + mod[1:2]) + mod[0:1]
    hb = h.astype(BF16)
    l0, l1, l2 = l0_ref[0], l1_ref[0], l2_ref[0]
    lm = jnp.maximum(jnp.maximum(l0, l1), l2)
    e0, e1, e2 = jnp.exp(l0 - lm), jnp.exp(l1 - lm), jnp.exp(l2 - lm)
    ob = (e0 * o0_ref[0] + e1 * o1_ref[0] + e2 * o2_ref[0]) / (e0 + e1 + e2)
    merged = jax.nn.sigmoid(_dot(hb, wg_ref[:, :D_MODEL])) * _dot(oa_ref[0], pa_ref[...])
    merged = merged + jax.nn.sigmoid(_dot(hb, wg_ref[:, D_MODEL:])) * _dot(ob.astype(BF16), pb_ref[...])
    x1_ref[0] = x + mod[2:3] * _dot(merged.astype(BF16), wo_ref[...])


def _out_call(x, mod, w, oa, dil_outs):
    B, S, _ = x.shape
    T = ROW_TILE
    row = lambda c: pl.BlockSpec((1, T, c), lambda b, i: (b, i, 0))
    os_ = [o for o, _ in dil_outs]
    ls_ = [l for _, l in dil_outs]
    return pl.pallas_call(
        _out_kernel,
        grid=(B, S // T),
        in_specs=[
            row(D_MODEL),
            pl.BlockSpec((1, 8, D_MODEL), lambda b, i: (b, 0, 0)),
            _const_spec((1, D_MODEL)),
            _const_spec((D_MODEL, 2 * D_MODEL)),
            row(MLA_HEADS * V_DIM),
            row(DIL_W), row(DIL_W), row(DIL_W), row(DIL_W), row(DIL_W), row(DIL_W),
            _const_spec((MLA_HEADS * V_DIM, D_MODEL)),
            _const_spec((DIL_W, D_MODEL)),
            _const_spec((D_MODEL, D_MODEL)),
        ],
        out_specs=row(D_MODEL),
        out_shape=jax.ShapeDtypeStruct((B, S, D_MODEL), F32),
        compiler_params=_params(("parallel", "parallel")),
        name="out_proj",
    )(x, mod, w["g1"], w["wg"], oa, *os_, *ls_, w["pa"], w["pb"], w["wo"])


def _ffn_kernel(xc_ref, xp_ref, xn_ref, mod_ref, g2_ref, wu_ref, cw_ref, cb_ref, wdn_ref, gf_ref, y_ref,
                h2s, a_s):
    T = xc_ref.shape[1]
    H = FFN_HALO
    mod = mod_ref[0]
    g2 = g2_ref[...]
    i = pl.program_id(1)
    n = pl.num_programs(1)

    def modnorm(v):
        return _rms(v, g2) * (1.0 + mod[4:5]) + mod[3:4]

    x1 = xc_ref[0]
    h2s[0:H] = jnp.where(i > 0, modnorm(xp_ref[0]), 0.0).astype(BF16)
    h2s[H:H + T] = modnorm(x1).astype(BF16)
    h2s[H + T:] = jnp.where(i < n - 1, modnorm(xn_ref[0]), 0.0).astype(BF16)

    rows = T + 2 * H
    for c in range(D_FF // FF_CHUNK):
        cs = slice(c * FF_CHUNK, (c + 1) * FF_CHUNK)
        ul = _dot(h2s[...], wu_ref[:, cs])
        gv = _dot(h2s[H:H + T], wu_ref[:, D_FF + c * FF_CHUNK:D_FF + (c + 1) * FF_CHUNK])
        u = (cb_ref[:, cs]
             + pltpu.roll(ul, 1, 0)[H:H + T] * cw_ref[0:1, cs]
             + ul[H:H + T] * cw_ref[1:2, cs]
             + pltpu.roll(ul, rows - 1, 0)[H:H + T] * cw_ref[2:3, cs])
        a_s[:, cs] = (jax.nn.gelu(u, approximate=True) * gv).astype(BF16)

    x2 = x1 + mod[5:6] * _dot(a_s[...], wdn_ref[...])
    y_ref[0] = _rms(x2, gf_ref[...])


def _ffn_call(x1, mod, w):
    B, S, _ = x1.shape
    T = ROW_TILE
    H = FFN_HALO
    per = T // H
    n_h = S // H
    return pl.pallas_call(
        _ffn_kernel,
        grid=(B, S // T),
        in_specs=[
            pl.BlockSpec((1, T, D_MODEL), lambda b, i: (b, i, 0)),
            pl.BlockSpec((1, H, D_MODEL), lambda b, i: (b, jnp.maximum(i * per - 1, 0), 0)),
            pl.BlockSpec((1, H, D_MODEL), lambda b, i: (b, jnp.minimum((i + 1) * per, n_h - 1), 0)),
            pl.BlockSpec((1, 8, D_MODEL), lambda b, i: (b, 0, 0)),
            _const_spec((1, D_MODEL)),
            _const_spec((D_MODEL, 2 * D_FF)),
            _const_spec((3, D_FF)),
            _const_spec((1, D_FF)),
            _const_spec((D_FF, D_MODEL)),
            _const_spec((1, D_MODEL)),
        ],
        out_specs=pl.BlockSpec((1, T, D_MODEL), lambda b, i: (b, i, 0)),
        out_shape=jax.ShapeDtypeStruct((B, S, D_MODEL), F32),
        scratch_shapes=[pltpu.VMEM((T + 2 * H, D_MODEL), BF16), pltpu.VMEM((T, D_FF), BF16)],
        compiler_params=_params(("parallel", "parallel")),
        name="conv_ffn",
    )(x1, x1, x1, mod, w["g2"], w["wu"], w["cw"], w["cb"], w["wdn"], w["gf"])


def _rope_tables(seq_len):
    inv = ROPE_THETA ** (-jnp.arange(0, QK_ROPE, 2, dtype=F32) / QK_ROPE)
    ang = jnp.arange(seq_len, dtype=F32)[:, None] * inv[None, :]
    cos, sin = jnp.cos(ang), jnp.sin(ang)
    zeros_l = jnp.zeros((seq_len, QK_NOPE), F32)
    zeros_r = jnp.zeros((seq_len, LANES - QK_NOPE - QK_ROPE), F32)
    ones_l = jnp.ones((seq_len, QK_NOPE), F32)
    scale = (QK_NOPE + QK_ROPE) ** -0.5 * math.log2(math.e)
    t1k = jnp.concatenate([zeros_l, cos, cos, zeros_r], axis=1)
    t2 = jnp.concatenate([zeros_l, -sin, sin, zeros_r], axis=1)
    t1q = jnp.concatenate([ones_l, cos, cos, zeros_r], axis=1) * scale
    return t1q, t2 * scale, t1k, t2


def _prep_weights(norm1_g, w_in, q_norm_g, kv_norm_g, w_uq, w_ukv, p_a, p_b, w_out, norm2_g, w_up,
                  conv_w, conv_b, w_down, normf_g):
    half = QK_ROPE // 2
    c0 = Q_LORA + KV_LORA
    kr = w_in[:, c0:c0 + QK_ROPE]
    kr_chunk = jnp.concatenate(
        [jnp.zeros((D_MODEL, QK_NOPE), F32), kr, kr[:, half:], kr[:, :half]], axis=1)
    d0 = c0 + QK_ROPE
    wa = jnp.concatenate([w_in[:, :c0], kr_chunk], axis=1)
    wd = w_in[:, d0:d0 + ZD_COLS]
    wg = w_in[:, d0 + ZD_COLS:]
    uq = w_uq.reshape(Q_LORA, MLA_HEADS, QK_NOPE + QK_ROPE)
    rope = uq[:, :, QK_NOPE:]
    wq = jnp.concatenate([uq, rope[:, :, half:], rope[:, :, :half]], axis=2).reshape(Q_LORA, -1)
    ukv = w_ukv.reshape(KV_LORA, MLA_HEADS, QK_NOPE + V_DIM)
    wk = jnp.concatenate([ukv[:, :, :QK_NOPE], jnp.zeros((KV_LORA, MLA_HEADS, HEAD_PAD - QK_NOPE), F32)],
                         axis=2).reshape(KV_LORA, -1)
    wv = ukv[:, :, QK_NOPE:].reshape(KV_LORA, -1)
    b = lambda a: a.astype(BF16)
    return dict(
        g1=norm1_g.reshape(1, -1), wa=b(wa), wd=b(wd), wg=b(wg),
        qg=q_norm_g.reshape(1, -1), kvg=kv_norm_g.reshape(1, -1),
        wq=b(wq), wk=b(wk), wvt=b(wv.T), pa=b(p_a), pb=b(p_b), wo=b(w_out),
        g2=norm2_g.reshape(1, -1), wu=b(w_up), cw=conv_w, cb=conv_b.reshape(1, -1), wdn=b(w_down),
        gf=normf_g.reshape(1, -1),
    )


def _trunk(x, mod, w):
    S = x.shape[1]
    assert S % DIL_TILE == 0 and S % ROW_TILE == 0 and S % (MLA_UNROLL * MLA_TK) == 0 and S % MLA_TQ == 0
    q, k, vt, zd = _in_call(x, mod, w, _rope_tables(S))
    oa = _mla_call(q, k, vt)
    dil_outs = [_dil_call(zd, g, dil) for g, (_, dil) in enumerate(DIL_CONFIGS)]
    x1 = _out_call(x, mod, w, oa, dil_outs)
    return _ffn_call(x1, mod, w)


def _mods(c_list, ada_w, ada_b):
    c_all = jnp.concatenate(c_list, axis=0)
    n = c_all.shape[0]
    rows = -(-n // 8) * 8
    c_all = jnp.pad(c_all, ((0, rows - n), (0, 0)))
    mod = _mod_call(c_all, ada_w, ada_b.reshape(1, -1)).reshape(rows, 6, D_MODEL)
    mod = jnp.pad(mod, ((0, 0), (0, 2), (0, 0)))
    out, o = [], 0
    for c in c_list:
        out.append(mod[o:o + c.shape[0]])
        o += c.shape[0]
    return out


def kernel(x_prompt, x_sample, c_prompt, c_sample, ada_w, ada_b, norm1_g, w_in, q_norm_g, kv_norm_g, w_uq, w_ukv, p_a, p_b, w_out, norm2_g, w_up, conv_w, conv_b, w_down, normf_g):
    assert ada_w.shape[0] == 1, "single layer"
    w = _prep_weights(norm1_g[0], w_in[0], q_norm_g[0], kv_norm_g[0], w_uq[0], w_ukv[0], p_a[0], p_b[0],
                      w_out[0], norm2_g[0], w_up[0], conv_w[0], conv_b[0], w_down[0], normf_g)
    mod_p, mod_s = _mods([c_prompt, c_sample], ada_w[0], ada_b[0])
    return (_trunk(x_prompt, mod_p, w), _trunk(x_sample, mod_s, w))
```

```python
import functools
import math

import jax
import jax.numpy as jnp
import numpy as np
from jax import lax
from jax.experimental import pallas as pl
from jax.experimental.pallas import tpu as pltpu

F32 = jnp.float32
BF16 = jnp.bfloat16

D_MODEL = 1024
MLA_HEADS = 16
QK_NOPE = 64
QK_ROPE = 32
V_DIM = 64
Q_LORA = 384
KV_LORA = 256
ROPE_THETA = 10000.0
DIL_CONFIGS = ((128, 1), (512, 4), (2048, 16))
DIL_HEADS_PER_GROUP = 4
DIL_HEADS = DIL_HEADS_PER_GROUP * len(DIL_CONFIGS)
DIL_HEAD_DIM = 64
D_FF = 2816
EPS = 1e-6

LANES = 128
HEAD_PAD = 128
DIL_W = DIL_HEADS_PER_GROUP * DIL_HEAD_DIM
N_SIDE = 64
Q_BLK = 128
K_WIN = Q_BLK + 2 * N_SIDE
DIL_TILE = 2048
ROW_TILE = 512
FFN_HALO = 16
FF_CHUNK = 256
MLA_TQ = 512
MLA_TK = 512
MLA_UNROLL = 8
MLA_LROWS = 16
NEG = -1e30
VMEM_LIMIT = 56 * 1024 * 1024

ZA_COLS = Q_LORA + KV_LORA + LANES
ZD_COLS = 3 * DIL_HEADS * DIL_HEAD_DIM


def _params(sem):
    return pltpu.CompilerParams(dimension_semantics=sem, vmem_limit_bytes=VMEM_LIMIT)


def _const_spec(shape):
    nd = len(shape)
    return pl.BlockSpec(shape, lambda *_: (0,) * nd, pipeline_mode=pl.Buffered(1))


def _rms(x, g):
    return x * lax.rsqrt(jnp.mean(x * x, axis=-1, keepdims=True) + EPS) * g


def _dot(a, b):
    return jnp.dot(a, b, preferred_element_type=F32)


def _dot_nt(a, b):
    return lax.dot_general(a, b, (((1,), (1,)), ((), ())), preferred_element_type=F32)


def _mod_kernel(c_ref, w_ref, b_ref, o_ref):
    c = c_ref[...]
    a = c * jax.nn.sigmoid(c)
    a_hi = a.astype(BF16)
    a_lo = (a - a_hi.astype(F32)).astype(BF16)
    w = w_ref[...]
    w_hi = w.astype(BF16)
    w_lo = (w - w_hi.astype(F32)).astype(BF16)
    o_ref[...] = _dot(a_hi, w_hi) + _dot(a_lo, w_hi) + _dot(a_hi, w_lo) + b_ref[...]


def _mod_call(c_all, ada_w, ada_b):
    rows = c_all.shape[0]
    n = ada_w.shape[1]
    tn = 1536
    return pl.pallas_call(
        _mod_kernel,
        grid=(n // tn,),
        in_specs=[
            pl.BlockSpec((rows, D_MODEL), lambda j: (0, 0)),
            pl.BlockSpec((D_MODEL, tn), lambda j: (0, j)),
            pl.BlockSpec((1, tn), lambda j: (0, j)),
        ],
        out_specs=pl.BlockSpec((rows, tn), lambda j: (0, j)),
        out_shape=jax.ShapeDtypeStruct((rows, n), F32),
        compiler_params=_params(("arbitrary",)),
        name="mod",
    )(c_all, ada_w, ada_b)


def _in_kernel(x_ref, mod_ref, g1_ref, wa_ref, wd_ref, qg_ref, kvg_ref, wqt_ref, wk_ref, wvt_ref,
               t1q_ref, t2q_ref, t1k_ref, t2k_ref, qt_ref, k_ref, vt_ref, zd_ref):
    x = x_ref[0]
    mod = mod_ref[0]
    h = _rms(x, g1_ref[...]) * (1.0 + mod[1:2]) + mod[0:1]
    hb = h.astype(BF16)
    zd_ref[0] = _dot(hb, wd_ref[...]).astype(BF16)
    za = _dot(hb, wa_ref[...])
    cq = _rms(za[:, :Q_LORA], qg_ref[...]).astype(BF16)
    ckv = _rms(za[:, Q_LORA:Q_LORA + KV_LORA], kvg_ref[...]).astype(BF16)
    kr = za[:, Q_LORA + KV_LORA:]
    kr = kr * t1k_ref[...] + pltpu.roll(kr, LANES - QK_ROPE, 1) * t2k_ref[...]
    vt_ref[0] = _dot_nt(wvt_ref[...], ckv).astype(BF16)
    qtf = _dot_nt(wqt_ref[...], cq)
    kf = _dot(ckv, wk_ref[...])
    t1q = t1q_ref[...]
    t2q = t2q_ref[...]
    for hd in range(MLA_HEADS):
        sl = slice(hd * HEAD_PAD, (hd + 1) * HEAD_PAD)
        qh = qtf[sl]
        qt_ref[0, sl, :] = (qh * t1q + pltpu.roll(qh, HEAD_PAD - QK_ROPE, 0) * t2q).astype(BF16)
        k_ref[0, :, sl] = (kf[:, sl] + kr).astype(BF16)


def _in_call(x, mod, w, tabs):
    B, S, _ = x.shape
    T = ROW_TILE
    row = lambda c: pl.BlockSpec((1, T, c), lambda b, i: (b, i, 0))
    tab = pl.BlockSpec((T, LANES), lambda b, i: (i, 0))
    tab_t = pl.BlockSpec((LANES, T), lambda b, i: (0, i))
    col = lambda r: pl.BlockSpec((1, r, T), lambda b, i: (b, 0, i))
    qk_cols = MLA_HEADS * HEAD_PAD
    return pl.pallas_call(
        _in_kernel,
        grid=(B, S // T),
        in_specs=[
            row(D_MODEL),
            pl.BlockSpec((1, 8, D_MODEL), lambda b, i: (b, 0, 0)),
            _const_spec((1, D_MODEL)),
            _const_spec((D_MODEL, ZA_COLS)),
            _const_spec((D_MODEL, ZD_COLS)),
            _const_spec((1, Q_LORA)),
            _const_spec((1, KV_LORA)),
            _const_spec((qk_cols, Q_LORA)),
            _const_spec((KV_LORA, qk_cols)),
            _const_spec((MLA_HEADS * V_DIM, KV_LORA)),
            tab_t, tab_t, tab, tab,
        ],
        out_specs=[col(qk_cols), row(qk_cols), col(MLA_HEADS * V_DIM), row(ZD_COLS)],
        out_shape=[
            jax.ShapeDtypeStruct((B, qk_cols, S), BF16),
            jax.ShapeDtypeStruct((B, S, qk_cols), BF16),
            jax.ShapeDtypeStruct((B, MLA_HEADS * V_DIM, S), BF16),
            jax.ShapeDtypeStruct((B, S, ZD_COLS), BF16),
        ],
        compiler_params=_params(("parallel", "parallel")),
        name="in_proj",
    )(x, mod, w["g1"], w["wa"], w["wd"], w["qg"], w["kvg"], w["wqt"], w["wk"], w["wvt"], *tabs)


def _mla_kernel(qt_ref, k_ref, vt_ref, o_ref, s0_ref, s1_ref, acc_ref, *, n_q, n_kv, unroll):
    tq, tk = MLA_TQ, MLA_TK
    ones = jnp.ones((MLA_LROWS, tk), BF16)
    slots = (s0_ref, s1_ref)
    n_chunks = n_q * n_kv

    def q_cols(c):
        return pl.ds(pl.multiple_of((c // n_kv) * tq, tq), tq)

    def kv_rows(c):
        return pl.ds(pl.multiple_of((c % n_kv) * tk, tk), tk)

    def scores(c, slot):
        mx = []
        for hh in range(2):
            sl = slice(hh * HEAD_PAD, (hh + 1) * HEAD_PAD)
            st = _dot(k_ref[0, kv_rows(c), sl], qt_ref[0, sl, q_cols(c)])
            slots[slot][hh] = st
            mx.append(jnp.max(st, axis=0, keepdims=True))
        return tuple(mx)

    def accumulate(c, slot, m, mx):
        first = (c % n_kv) == 0
        out = []
        for hh in range(2):
            m_old = jnp.where(first, -jnp.inf, m[hh])
            m_new = jnp.maximum(m_old, mx[hh])
            alpha = jnp.exp2(m_old - m_new)
            pt = jnp.exp2(slots[slot][hh] - m_new).astype(BF16)
            lhs = jnp.concatenate([vt_ref[0, hh * V_DIM:(hh + 1) * V_DIM, kv_rows(c)], ones], axis=0)
            acc_ref[hh] = alpha * acc_ref[hh] + _dot(lhs, pt)
            out.append(m_new)
        return tuple(out)

    def finalize(c):
        ot = jnp.concatenate([acc_ref[hh, :V_DIM] / acc_ref[hh, V_DIM:V_DIM + 1] for hh in range(2)], axis=0)
        o_ref[0, q_cols(c), :] = ot.T.astype(BF16)

    def run(c0, mx, m, prefetch_last):
        for t in range(unroll):
            if t < unroll - 1 or prefetch_last:
                mx_next = scores(c0 + t + 1, (t + 1) % 2)
            m = accumulate(c0 + t, t % 2, m, mx)
            mx = mx_next
        return m, mx

    acc_ref[...] = jnp.zeros_like(acc_ref)
    m = tuple(jnp.full((1, tq), -jnp.inf, F32) for _ in range(2))
    mx = scores(0, 0)

    def body(i, carry):
        c0 = unroll * i
        m, mx = run(c0, carry[1], carry[0], True)
        last = c0 + unroll - 1

        @pl.when(last % n_kv == n_kv - 1)
        def _():
            finalize(last)

        return m, mx

    m, mx = lax.fori_loop(0, n_chunks // unroll - 1, body, (m, mx))
    run(n_chunks - unroll, mx, m, False)
    finalize(n_chunks - 1)


def _mla_call(qt, k, vt):
    B, S, _ = k.shape
    tq, tk = MLA_TQ, MLA_TK
    return pl.pallas_call(
        functools.partial(_mla_kernel, n_q=S // tq, n_kv=S // tk, unroll=math.gcd(MLA_UNROLL, S // tk)),
        grid=(B, MLA_HEADS // 2),
        in_specs=[
            pl.BlockSpec((1, 2 * HEAD_PAD, S), lambda b, h: (b, h, 0)),
            pl.BlockSpec((1, S, 2 * HEAD_PAD), lambda b, h: (b, 0, h)),
            pl.BlockSpec((1, 2 * V_DIM, S), lambda b, h: (b, h, 0)),
        ],
        out_specs=pl.BlockSpec((1, S, 2 * V_DIM), lambda b, h: (b, 0, h)),
        out_shape=jax.ShapeDtypeStruct((B, S, MLA_HEADS * V_DIM), BF16),
        scratch_shapes=[
            pltpu.VMEM((2, tk, tq), F32),
            pltpu.VMEM((2, tk, tq), F32),
            pltpu.VMEM((2, V_DIM + MLA_LROWS, tq), F32),
        ],
        compiler_params=_params(("parallel", "parallel")),
        name="mla_attn",
    )(qt, k, vt)


def _dil_kernel(q_ref, kp_ref, kc_ref, kn_ref, vp_ref, vc_ref, vn_ref, o_ref, lse_ref,
                qs, ks, vs, os_, ls_, *, dil, group, seq_len):
    halo = N_SIDE * dil
    n_blk = DIL_TILE // (Q_BLK * dil)
    sc_len = seq_len // dil
    i0 = pl.program_id(1) * (DIL_TILE // dil)

    for pair in range(2):
        cs = slice(pair * LANES, (pair + 1) * LANES)
        qs[pair] = q_ref[0, :, cs].astype(F32)
        for dst, (p_ref, c_ref, n_ref) in ((ks, (kp_ref, kc_ref, kn_ref)), (vs, (vp_ref, vc_ref, vn_ref))):
            dst[pair, 0:halo] = p_ref[0, :, cs].astype(F32)
            dst[pair, halo:halo + DIL_TILE] = c_ref[0, :, cs].astype(F32)
            dst[pair, halo + DIL_TILE:] = n_ref[0, :, cs].astype(F32)

    col = lax.broadcasted_iota(jnp.int32, (Q_BLK, K_WIN), 1)
    rowi = lax.broadcasted_iota(jnp.int32, (Q_BLK, K_WIN), 0)
    rel = col - N_SIDE - rowi
    absrel = jnp.abs(rel).astype(F32)
    band = jnp.abs(rel) <= N_SIDE
    lane = lax.broadcasted_iota(jnp.int32, (Q_BLK, LANES), 1)
    first_head = lane < DIL_HEAD_DIM
    scale = DIL_HEAD_DIM ** -0.5

    for jb in range(n_blk):
        kc0 = i0 + (jb * Q_BLK - N_SIDE)
        valid = band & (col >= -kc0) & (col < sc_len - kc0)
        for r in range(dil):
            start = jb * Q_BLK * dil + r
            q_rows = pl.ds(start, Q_BLK, stride=dil) if dil > 1 else pl.ds(start, Q_BLK)
            k_rows = pl.ds(start, K_WIN, stride=dil) if dil > 1 else pl.ds(start, K_WIN)
            for pair in range(2):
                qp = qs[pair, q_rows, :] * scale
                kp = ks[pair, k_rows, :].astype(BF16)
                vp = vs[pair, k_rows, :].astype(BF16)
                o_acc = None
                lse_acc = None
                for hh in range(2):
                    head = group * DIL_HEADS_PER_GROUP + pair * 2 + hh
                    slope = 2.0 ** (-8.0 * (head + 1) / DIL_HEADS)
                    sel = first_head if hh == 0 else jnp.logical_not(first_head)
                    qm = jnp.where(sel, qp, 0.0).astype(BF16)
                    s = _dot_nt(qm, kp) - absrel * (slope * dil)
                    s = jnp.where(valid, s, NEG)
                    m = jnp.max(s, axis=1, keepdims=True)
                    p = jnp.exp(s - m)
                    l = jnp.sum(p, axis=1, keepdims=True)
                    o = _dot(p.astype(BF16), vp) / l
                    lse = jnp.broadcast_to(m + jnp.log(l), (Q_BLK, LANES))
                    o_acc = o if hh == 0 else jnp.where(sel, o, o_acc)
                    lse_acc = lse if hh == 0 else jnp.where(sel, lse, lse_acc)
                os_[pair, q_rows, :] = o_acc
                ls_[pair, q_rows, :] = lse_acc

    for pair in range(2):
        cs = slice(pair * LANES, (pair + 1) * LANES)
        o_ref[0, :, cs] = os_[pair]
        lse_ref[0, :, cs] = ls_[pair]


def _dil_call(zd, group, dil):
    B, S, _ = zd.shape
    halo = N_SIDE * dil
    per = DIL_TILE // halo
    n_halo = S // halo
    g = group
    cur = lambda c: pl.BlockSpec((1, DIL_TILE, DIL_W), lambda b, i: (b, i, c))
    prev = lambda c: pl.BlockSpec((1, halo, DIL_W), lambda b, i: (b, jnp.maximum(i * per - 1, 0), c))
    nxt = lambda c: pl.BlockSpec((1, halo, DIL_W), lambda b, i: (b, jnp.minimum((i + 1) * per, n_halo - 1), c))
    n_g = len(DIL_CONFIGS)
    out_spec = pl.BlockSpec((1, DIL_TILE, DIL_W), lambda b, i: (b, i, 0))
    return pl.pallas_call(
        functools.partial(_dil_kernel, dil=dil, group=group, seq_len=S),
        grid=(B, S // DIL_TILE),
        in_specs=[cur(g), prev(n_g + g), cur(n_g + g), nxt(n_g + g),
                  prev(2 * n_g + g), cur(2 * n_g + g), nxt(2 * n_g + g)],
        out_specs=[out_spec, out_spec],
        out_shape=[jax.ShapeDtypeStruct((B, S, DIL_W), F32), jax.ShapeDtypeStruct((B, S, DIL_W), F32)],
        scratch_shapes=[
            pltpu.VMEM((2, DIL_TILE, LANES), F32),
            pltpu.VMEM((2, DIL_TILE + 2 * halo, LANES), F32),
            pltpu.VMEM((2, DIL_TILE + 2 * halo, LANES), F32),
            pltpu.VMEM((2, DIL_TILE, LANES), F32),
            pltpu.VMEM((2, DIL_TILE, LANES), F32),
        ],
        compiler_params=_params(("parallel", "parallel")),
        name=f"dil_attn_d{dil}",
    )(zd, zd, zd, zd, zd, zd, zd)


def _out_kernel(x_ref, mod_ref, g1_ref, wg_ref, oa_ref, o0_ref, o1_ref, o2_ref, l0_ref, l1_ref, l2_ref,
                pa_ref, pb_ref, wo_ref, x1_ref):
    x = x_ref[0]
    mod = mod_ref[0]
    h = _rms(x, g1_ref[...]) * (1.0 + mod[1:2]) + mod[0:1]
    hb = h.astype(BF16)
    l0, l1, l2 = l0_ref[0], l1_ref[0], l2_ref[0]
    lm = jnp.maximum(jnp.maximum(l0, l1), l2)
    e0, e1, e2 = jnp.exp(l0 - lm), jnp.exp(l1 - lm), jnp.exp(l2 - lm)
    ob = (e0 * o0_ref[0] + e1 * o1_ref[0] + e2 * o2_ref[0]) / (e0 + e1 + e2)
    merged = jax.nn.sigmoid(_dot(hb, wg_ref[:, :D_MODEL])) * _dot(oa_ref[0], pa_ref[...])
    merged = merged + jax.nn.sigmoid(_dot(hb, wg_ref[:, D_MODEL:])) * _dot(ob.astype(BF16), pb_ref[...])
    x1_ref[0] = x + mod[2:3] * _dot(merged.astype(BF16), wo_ref[...])


def _out_call(x, mod, w, oa, dil_outs):
    B, S, _ = x.shape
    T = ROW_TILE
    row = lambda c: pl.BlockSpec((1, T, c), lambda b, i: (b, i, 0))
    os_ = [o for o, _ in dil_outs]
    ls_ = [l for _, l in dil_outs]
    return pl.pallas_call(
        _out_kernel,
        grid=(B, S // T),
        in_specs=[
            row(D_MODEL),
            pl.BlockSpec((1, 8, D_MODEL), lambda b, i: (b, 0, 0)),
            _const_spec((1, D_MODEL)),
            _const_spec((D_MODEL, 2 * D_MODEL)),
            row(MLA_HEADS * V_DIM),
            row(DIL_W), row(DIL_W), row(DIL_W), row(DIL_W), row(DIL_W), row(DIL_W),
            _const_spec((MLA_HEADS * V_DIM, D_MODEL)),
            _const_spec((DIL_W, D_MODEL)),
            _const_spec((D_MODEL, D_MODEL)),
        ],
        out_specs=row(D_MODEL),
        out_shape=jax.ShapeDtypeStruct((B, S, D_MODEL), F32),
        compiler_params=_params(("parallel", "parallel")),
        name="out_proj",
    )(x, mod, w["g1"], w["wg"], oa, *os_, *ls_, w["pa"], w["pb"], w["wo"])


def _ffn_kernel(xc_ref, xp_ref, xn_ref, mod_ref, g2_ref, wu_ref, cw_ref, cb_ref, wdn_ref, gf_ref, y_ref,
                h2s, a_s):
    T = xc_ref.shape[1]
    H = FFN_HALO
    mod = mod_ref[0]
    g2 = g2_ref[...]
    i = pl.program_id(1)
    n = pl.num_programs(1)

    def modnorm(v):
        return _rms(v, g2) * (1.0 + mod[4:5]) + mod[3:4]

    x1 = xc_ref[0]
    h2s[0:H] = jnp.where(i > 0, modnorm(xp_ref[0]), 0.0).astype(BF16)
    h2s[H:H + T] = modnorm(x1).astype(BF16)
    h2s[H + T:] = jnp.where(i < n - 1, modnorm(xn_ref[0]), 0.0).astype(BF16)

    rows = T + 2 * H
    for c in range(D_FF // FF_CHUNK):
        cs = slice(c * FF_CHUNK, (c + 1) * FF_CHUNK)
        ul = _dot(h2s[...], wu_ref[:, cs])
        gv = _dot(h2s[H:H + T], wu_ref[:, D_FF + c * FF_CHUNK:D_FF + (c + 1) * FF_CHUNK])
        u = (cb_ref[:, cs]
             + pltpu.roll(ul, 1, 0)[H:H + T] * cw_ref[0:1, cs]
             + ul[H:H + T] * cw_ref[1:2, cs]
             + pltpu.roll(ul, rows - 1, 0)[H:H + T] * cw_ref[2:3, cs])
        a_s[:, cs] = (jax.nn.gelu(u, approximate=True) * gv).astype(BF16)

    x2 = x1 + mod[5:6] * _dot(a_s[...], wdn_ref[...])
    y_ref[0] = _rms(x2, gf_ref[...])


def _ffn_call(x1, mod, w):
    B, S, _ = x1.shape
    T = ROW_TILE
    H = FFN_HALO
    per = T // H
    n_h = S // H
    return pl.pallas_call(
        _ffn_kernel,
        grid=(B, S // T),
        in_specs=[
            pl.BlockSpec((1, T, D_MODEL), lambda b, i: (b, i, 0)),
            pl.BlockSpec((1, H, D_MODEL), lambda b, i: (b, jnp.maximum(i * per - 1, 0), 0)),
            pl.BlockSpec((1, H, D_MODEL), lambda b, i: (b, jnp.minimum((i + 1) * per, n_h - 1), 0)),
            pl.BlockSpec((1, 8, D_MODEL), lambda b, i: (b, 0, 0)),
            _const_spec((1, D_MODEL)),
            _const_spec((D_MODEL, 2 * D_FF)),
            _const_spec((3, D_FF)),
            _const_spec((1, D_FF)),
            _const_spec((D_FF, D_MODEL)),
            _const_spec((1, D_MODEL)),
        ],
        out_specs=pl.BlockSpec((1, T, D_MODEL), lambda b, i: (b, i, 0)),
        out_shape=jax.ShapeDtypeStruct((B, S, D_MODEL), F32),
        scratch_shapes=[pltpu.VMEM((T + 2 * H, D_MODEL), BF16), pltpu.VMEM((T, D_FF), BF16)],
        compiler_params=_params(("parallel", "parallel")),
        name="conv_ffn",
    )(x1, x1, x1, mod, w["g2"], w["wu"], w["cw"], w["cb"], w["wdn"], w["gf"])


def _rope_tables(seq_len):
    inv = ROPE_THETA ** (-jnp.arange(0, QK_ROPE, 2, dtype=F32) / QK_ROPE)
    ang = jnp.arange(seq_len, dtype=F32)[:, None] * inv[None, :]
    cos, sin = jnp.cos(ang), jnp.sin(ang)
    zeros_l = jnp.zeros((seq_len, QK_NOPE), F32)
    zeros_r = jnp.zeros((seq_len, LANES - QK_NOPE - QK_ROPE), F32)
    ones_l = jnp.ones((seq_len, QK_NOPE), F32)
    scale = (QK_NOPE + QK_ROPE) ** -0.5 * math.log2(math.e)
    t1k = jnp.concatenate([zeros_l, cos, cos, zeros_r], axis=1)
    t2 = jnp.concatenate([zeros_l, -sin, sin, zeros_r], axis=1)
    t1q = jnp.concatenate([ones_l, cos, cos, zeros_r], axis=1) * scale
    return t1q.T, (t2 * scale).T, t1k, t2


def _prep_weights(norm1_g, w_in, q_norm_g, kv_norm_g, w_uq, w_ukv, p_a, p_b, w_out, norm2_g, w_up,
                  conv_w, conv_b, w_down, normf_g):
    half = QK_ROPE // 2
    c0 = Q_LORA + KV_LORA
    kr = w_in[:, c0:c0 + QK_ROPE]
    kr_chunk = jnp.concatenate(
        [jnp.zeros((D_MODEL, QK_NOPE), F32), kr, kr[:, half:], kr[:, :half]], axis=1)
    d0 = c0 + QK_ROPE
    wa = jnp.concatenate([w_in[:, :c0], kr_chunk], axis=1)
    wd = w_in[:, d0:d0 + ZD_COLS]
    wg = w_in[:, d0 + ZD_COLS:]
    uq = w_uq.reshape(Q_LORA, MLA_HEADS, QK_NOPE + QK_ROPE)
    rope = uq[:, :, QK_NOPE:]
    wq = jnp.concatenate([uq, rope[:, :, half:], rope[:, :, :half]], axis=2).reshape(Q_LORA, -1)
    ukv = w_ukv.reshape(KV_LORA, MLA_HEADS, QK_NOPE + V_DIM)
    wk = jnp.concatenate([ukv[:, :, :QK_NOPE], jnp.zeros((KV_LORA, MLA_HEADS, HEAD_PAD - QK_NOPE), F32)],
                         axis=2).reshape(KV_LORA, -1)
    wv = ukv[:, :, QK_NOPE:].reshape(KV_LORA, -1)
    b = lambda a: a.astype(BF16)
    return dict(
        g1=norm1_g.reshape(1, -1), wa=b(wa), wd=b(wd), wg=b(wg),
        qg=q_norm_g.reshape(1, -1), kvg=kv_norm_g.reshape(1, -1),
        wqt=b(wq.T), wk=b(wk), wvt=b(wv.T), pa=b(p_a), pb=b(p_b), wo=b(w_out),
        g2=norm2_g.reshape(1, -1), wu=b(w_up), cw=conv_w, cb=conv_b.reshape(1, -1), wdn=b(w_down),
        gf=normf_g.reshape(1, -1),
    )


def _trunk(x, mod, w):
    S = x.shape[1]
    assert S % DIL_TILE == 0 and S % ROW_TILE == 0 and S % MLA_TK == 0 and S % MLA_TQ == 0
    assert (S // MLA_TK) % 2 == 0
    qt, k, vt, zd = _in_call(x, mod, w, _rope_tables(S))
    oa = _mla_call(qt, k, vt)
    dil_outs = [_dil_call(zd, g, dil) for g, (_, dil) in enumerate(DIL_CONFIGS)]
    x1 = _out_call(x, mod, w, oa, dil_outs)
    return _ffn_call(x1, mod, w)


def _mods(c_list, ada_w, ada_b):
    c_all = jnp.concatenate(c_list, axis=0)
    n = c_all.shape[0]
    rows = -(-n // 8) * 8
    c_all = jnp.pad(c_all, ((0, rows - n), (0, 0)))
    mod = _mod_call(c_all, ada_w, ada_b.reshape(1, -1)).reshape(rows, 6, D_MODEL)
    mod = jnp.pad(mod, ((0, 0), (0, 2), (0, 0)))
    out, o = [], 0
    for c in c_list:
        out.append(mod[o:o + c.shape[0]])
        o += c.shape[0]
    return out


def kernel(x_prompt, x_sample, c_prompt, c_sample, ada_w, ada_b, norm1_g, w_in, q_norm_g, kv_norm_g, w_uq, w_ukv, p_a, p_b, w_out, norm2_g, w_up, conv_w, conv_b, w_down, normf_g):
    assert ada_w.shape[0] == 1, "single layer"
    w = _prep_weights(norm1_g[0], w_in[0], q_norm_g[0], kv_norm_g[0], w_uq[0], w_ukv[0], p_a[0], p_b[0],
                      w_out[0], norm2_g[0], w_up[0], conv_w[0], conv_b[0], w_down[0], normf_g)
    mod_p, mod_s = _mods([c_prompt, c_sample], ada_w[0], ada_b[0])
    return (_trunk(x_prompt, mod_p, w), _trunk(x_sample, mod_s, w))
```

```python
import functools
import math

import jax
import jax.numpy as jnp
import numpy as np
from jax import lax
from jax.experimental import pallas as pl
from jax.experimental.pallas import tpu as pltpu

F32 = jnp.float32
BF16 = jnp.bfloat16

D_MODEL = 1024
MLA_HEADS = 16
QK_NOPE = 64
QK_ROPE = 32
V_DIM = 64
Q_LORA = 384
KV_LORA = 256
ROPE_THETA = 10000.0
DIL_CONFIGS = ((128, 1), (512, 4), (2048, 16))
DIL_HEADS_PER_GROUP = 4
DIL_HEADS = DIL_HEADS_PER_GROUP * len(DIL_CONFIGS)
DIL_HEAD_DIM = 64
D_FF = 2816
EPS = 1e-6

LANES = 128
HEAD_PAD = 128
DIL_W = DIL_HEADS_PER_GROUP * DIL_HEAD_DIM
N_SIDE = 64
Q_BLK = 128
K_WIN = Q_BLK + 2 * N_SIDE
DIL_TILE = 2048
ROW_TILE = 512
FFN_HALO = 16
FF_CHUNK = 256
MLA_TQ = 512
MLA_TK = 512
MLA_UNROLL = 8
MLA_LROWS = 16
NEG = -1e30
VMEM_LIMIT = 56 * 1024 * 1024

ZA_COLS = Q_LORA + KV_LORA + LANES
ZD_COLS = 3 * DIL_HEADS * DIL_HEAD_DIM


def _params(sem):
    return pltpu.CompilerParams(dimension_semantics=sem, vmem_limit_bytes=VMEM_LIMIT)


def _const_spec(shape):
    nd = len(shape)
    return pl.BlockSpec(shape, lambda *_: (0,) * nd, pipeline_mode=pl.Buffered(1))


def _rms(x, g):
    return x * lax.rsqrt(jnp.mean(x * x, axis=-1, keepdims=True) + EPS) * g


def _dot(a, b):
    return jnp.dot(a, b, preferred_element_type=F32)


def _dot_nt(a, b):
    return lax.dot_general(a, b, (((1,), (1,)), ((), ())), preferred_element_type=F32)


def _mod_kernel(c_ref, w_ref, b_ref, o_ref):
    c = c_ref[...]
    a = c * jax.nn.sigmoid(c)
    a_hi = a.astype(BF16)
    a_lo = (a - a_hi.astype(F32)).astype(BF16)
    w = w_ref[...]
    w_hi = w.astype(BF16)
    w_lo = (w - w_hi.astype(F32)).astype(BF16)
    o_ref[...] = _dot(a_hi, w_hi) + _dot(a_lo, w_hi) + _dot(a_hi, w_lo) + b_ref[...]


def _mod_call(c_all, ada_w, ada_b):
    rows = c_all.shape[0]
    n = ada_w.shape[1]
    tn = 1536
    return pl.pallas_call(
        _mod_kernel,
        grid=(n // tn,),
        in_specs=[
            pl.BlockSpec((rows, D_MODEL), lambda j: (0, 0)),
            pl.BlockSpec((D_MODEL, tn), lambda j: (0, j)),
            pl.BlockSpec((1, tn), lambda j: (0, j)),
        ],
        out_specs=pl.BlockSpec((rows, tn), lambda j: (0, j)),
        out_shape=jax.ShapeDtypeStruct((rows, n), F32),
        compiler_params=_params(("arbitrary",)),
        name="mod",
    )(c_all, ada_w, ada_b)


def _in_kernel(x_ref, mod_ref, g1_ref, wa_ref, wd_ref, qg_ref, kvg_ref, wqt_ref, wk_ref, wvt_ref,
               t1q_ref, t2q_ref, t1k_ref, t2k_ref, qt_ref, k_ref, vt_ref, zd_ref):
    x = x_ref[0]
    mod = mod_ref[0]
    h = _rms(x, g1_ref[...]) * (1.0 + mod[1:2]) + mod[0:1]
    hb = h.astype(BF16)
    zd_ref[0] = _dot(hb, wd_ref[...]).astype(BF16)
    za = _dot(hb, wa_ref[...])
    cq = _rms(za[:, :Q_LORA], qg_ref[...]).astype(BF16)
    ckv = _rms(za[:, Q_LORA:Q_LORA + KV_LORA], kvg_ref[...]).astype(BF16)
    kr = za[:, Q_LORA + KV_LORA:]
    kr = kr * t1k_ref[...] + pltpu.roll(kr, LANES - QK_ROPE, 1) * t2k_ref[...]
    vt_ref[0] = _dot_nt(wvt_ref[...], ckv).astype(BF16)
    qtf = _dot_nt(wqt_ref[...], cq)
    kf = _dot(ckv, wk_ref[...])
    t1q = t1q_ref[...]
    t2q = t2q_ref[...]
    for hd in range(MLA_HEADS):
        sl = slice(hd * HEAD_PAD, (hd + 1) * HEAD_PAD)
        qh = qtf[sl]
        qt_ref[0, sl, :] = (qh * t1q + pltpu.roll(qh, HEAD_PAD - QK_ROPE, 0) * t2q).astype(BF16)
        k_ref[0, :, sl] = (kf[:, sl] + kr).astype(BF16)


def _in_call(x, mod, w, tabs):
    B, S, _ = x.shape
    T = ROW_TILE
    row = lambda c: pl.BlockSpec((1, T, c), lambda b, i: (b, i, 0))
    tab = pl.BlockSpec((T, LANES), lambda b, i: (i, 0))
    tab_t = pl.BlockSpec((LANES, T), lambda b, i: (0, i))
    col = lambda r: pl.BlockSpec((1, r, T), lambda b, i: (b, 0, i))
    qk_cols = MLA_HEADS * HEAD_PAD
    return pl.pallas_call(
        _in_kernel,
        grid=(B, S // T),
        in_specs=[
            row(D_MODEL),
            pl.BlockSpec((1, 8, D_MODEL), lambda b, i: (b, 0, 0)),
            _const_spec((1, D_MODEL)),
            _const_spec((D_MODEL, ZA_COLS)),
            _const_spec((D_MODEL, ZD_COLS)),
            _const_spec((1, Q_LORA)),
            _const_spec((1, KV_LORA)),
            _const_spec((qk_cols, Q_LORA)),
            _const_spec((KV_LORA, qk_cols)),
            _const_spec((MLA_HEADS * V_DIM, KV_LORA)),
            tab_t, tab_t, tab, tab,
        ],
        out_specs=[col(qk_cols), row(qk_cols), col(MLA_HEADS * V_DIM), row(ZD_COLS)],
        out_shape=[
            jax.ShapeDtypeStruct((B, qk_cols, S), BF16),
            jax.ShapeDtypeStruct((B, S, qk_cols), BF16),
            jax.ShapeDtypeStruct((B, MLA_HEADS * V_DIM, S), BF16),
            jax.ShapeDtypeStruct((B, S, ZD_COLS), BF16),
        ],
        compiler_params=_params(("parallel", "parallel")),
        name="in_proj",
    )(x, mod, w["g1"], w["wa"], w["wd"], w["qg"], w["kvg"], w["wqt"], w["wk"], w["wvt"], *tabs)


def _mla_kernel(qt_ref, k_ref, vt_ref, o_ref, s0_ref, s1_ref, acc_ref, *, n_q, n_kv, unroll):
    tq, tk = MLA_TQ, MLA_TK
    ones = jnp.ones((MLA_LROWS, tk), BF16)
    slots = (s0_ref, s1_ref)
    n_chunks = n_q * n_kv

    def q_cols(c):
        return pl.ds(pl.multiple_of((c // n_kv) * tq, tq), tq)

    def kv_rows(c):
        return pl.ds(pl.multiple_of((c % n_kv) * tk, tk), tk)

    def scores(c, slot):
        mx = []
        for hh in range(2):
            sl = slice(hh * HEAD_PAD, (hh + 1) * HEAD_PAD)
            st = _dot(k_ref[0, kv_rows(c), sl], qt_ref[0, sl, q_cols(c)])
            slots[slot][hh, :, :tq] = st
            mx.append(jnp.max(st, axis=0, keepdims=True))
        return tuple(mx)

    def accumulate(c, slot, m, mx):
        first = (c % n_kv) == 0
        out = []
        for hh in range(2):
            m_old = jnp.where(first, -jnp.inf, m[hh])
            m_new = jnp.maximum(m_old, mx[hh])
            alpha = jnp.exp2(m_old - m_new)
            pt = jnp.exp2(slots[slot][hh, :, :tq] - m_new).astype(BF16)
            lhs = jnp.concatenate([vt_ref[0, hh * V_DIM:(hh + 1) * V_DIM, kv_rows(c)], ones], axis=0)
            acc_ref[hh] = alpha * acc_ref[hh] + _dot(lhs, pt)
            out.append(m_new)
        return tuple(out)

    def finalize(c):
        ot = jnp.concatenate([acc_ref[hh, :V_DIM] / acc_ref[hh, V_DIM:V_DIM + 1] for hh in range(2)], axis=0)
        o_ref[0, q_cols(c), :] = ot.T.astype(BF16)

    def run(c0, mx, m, prefetch_last):
        for t in range(unroll):
            if t < unroll - 1 or prefetch_last:
                mx_next = scores(c0 + t + 1, (t + 1) % 2)
            m = accumulate(c0 + t, t % 2, m, mx)
            mx = mx_next
        return m, mx

    acc_ref[...] = jnp.zeros_like(acc_ref)
    m = tuple(jnp.full((1, tq), -jnp.inf, F32) for _ in range(2))
    mx = scores(0, 0)

    def body(i, carry):
        c0 = unroll * i
        m, mx = run(c0, carry[1], carry[0], True)
        last = c0 + unroll - 1

        @pl.when(last % n_kv == n_kv - 1)
        def _():
            finalize(last)

        return m, mx

    m, mx = lax.fori_loop(0, n_chunks // unroll - 1, body, (m, mx))
    run(n_chunks - unroll, mx, m, False)
    finalize(n_chunks - 1)


def _mla_call(qt, k, vt):
    B, S, _ = k.shape
    tq, tk = MLA_TQ, MLA_TK
    return pl.pallas_call(
        functools.partial(_mla_kernel, n_q=S // tq, n_kv=S // tk, unroll=math.gcd(MLA_UNROLL, S // tk)),
        grid=(B, MLA_HEADS // 2),
        in_specs=[
            pl.BlockSpec((1, 2 * HEAD_PAD, S), lambda b, h: (b, h, 0)),
            pl.BlockSpec((1, S, 2 * HEAD_PAD), lambda b, h: (b, 0, h)),
            pl.BlockSpec((1, 2 * V_DIM, S), lambda b, h: (b, h, 0)),
        ],
        out_specs=pl.BlockSpec((1, S, 2 * V_DIM), lambda b, h: (b, 0, h)),
        out_shape=jax.ShapeDtypeStruct((B, S, MLA_HEADS * V_DIM), BF16),
        scratch_shapes=[
            pltpu.VMEM((2, tk, tq + LANES), F32),
            pltpu.VMEM((2, tk, tq + LANES), F32),
            pltpu.VMEM((2, V_DIM + MLA_LROWS, tq), F32),
        ],
        compiler_params=_params(("parallel", "parallel")),
        name="mla_attn",
    )(qt, k, vt)


def _dil_kernel(q_ref, kp_ref, kc_ref, kn_ref, vp_ref, vc_ref, vn_ref, o_ref, lse_ref,
                qs, ks, vs, os_, ls_, *, dil, group, seq_len):
    halo = N_SIDE * dil
    n_blk = DIL_TILE // (Q_BLK * dil)
    sc_len = seq_len // dil
    i0 = pl.program_id(1) * (DIL_TILE // dil)

    for pair in range(2):
        cs = slice(pair * LANES, (pair + 1) * LANES)
        qs[pair] = q_ref[0, :, cs].astype(F32)
        for dst, (p_ref, c_ref, n_ref) in ((ks, (kp_ref, kc_ref, kn_ref)), (vs, (vp_ref, vc_ref, vn_ref))):
            dst[pair, 0:halo] = p_ref[0, :, cs].astype(F32)
            dst[pair, halo:halo + DIL_TILE] = c_ref[0, :, cs].astype(F32)
            dst[pair, halo + DIL_TILE:] = n_ref[0, :, cs].astype(F32)

    col = lax.broadcasted_iota(jnp.int32, (Q_BLK, K_WIN), 1)
    rowi = lax.broadcasted_iota(jnp.int32, (Q_BLK, K_WIN), 0)
    rel = col - N_SIDE - rowi
    absrel = jnp.abs(rel).astype(F32)
    band = jnp.abs(rel) <= N_SIDE
    lane = lax.broadcasted_iota(jnp.int32, (Q_BLK, LANES), 1)
    first_head = lane < DIL_HEAD_DIM
    scale = DIL_HEAD_DIM ** -0.5

    for jb in range(n_blk):
        kc0 = i0 + (jb * Q_BLK - N_SIDE)
        valid = band & (col >= -kc0) & (col < sc_len - kc0)
        for r in range(dil):
            start = jb * Q_BLK * dil + r
            q_rows = pl.ds(start, Q_BLK, stride=dil) if dil > 1 else pl.ds(start, Q_BLK)
            k_rows = pl.ds(start, K_WIN, stride=dil) if dil > 1 else pl.ds(start, K_WIN)
            for pair in range(2):
                qp = qs[pair, q_rows, :] * scale
                kp = ks[pair, k_rows, :].astype(BF16)
                vp = vs[pair, k_rows, :].astype(BF16)
                o_acc = None
                lse_acc = None
                for hh in range(2):
                    head = group * DIL_HEADS_PER_GROUP + pair * 2 + hh
                    slope = 2.0 ** (-8.0 * (head + 1) / DIL_HEADS)
                    sel = first_head if hh == 0 else jnp.logical_not(first_head)
                    qm = jnp.where(sel, qp, 0.0).astype(BF16)
                    s = _dot_nt(qm, kp) - absrel * (slope * dil)
                    s = jnp.where(valid, s, NEG)
                    m = jnp.max(s, axis=1, keepdims=True)
                    p = jnp.exp(s - m)
                    l = jnp.sum(p, axis=1, keepdims=True)
                    o = _dot(p.astype(BF16), vp) / l
                    lse = jnp.broadcast_to(m + jnp.log(l), (Q_BLK, LANES))
                    o_acc = o if hh == 0 else jnp.where(sel, o, o_acc)
                    lse_acc = lse if hh == 0 else jnp.where(sel, lse, lse_acc)
                os_[pair, q_rows, :] = o_acc
                ls_[pair, q_rows, :] = lse_acc

    for pair in range(2):
        cs = slice(pair * LANES, (pair + 1) * LANES)
        o_ref[0, :, cs] = os_[pair]
        lse_ref[0, :, cs] = ls_[pair]


def _dil_call(zd, group, dil):
    B, S, _ = zd.shape
    halo = N_SIDE * dil
    per = DIL_TILE // halo
    n_halo = S // halo
    g = group
    cur = lambda c: pl.BlockSpec((1, DIL_TILE, DIL_W), lambda b, i: (b, i, c))
    prev = lambda c: pl.BlockSpec((1, halo, DIL_W), lambda b, i: (b, jnp.maximum(i * per - 1, 0), c))
    nxt = lambda c: pl.BlockSpec((1, halo, DIL_W), lambda b, i: (b, jnp.minimum((i + 1) * per, n_halo - 1), c))
    n_g = len(DIL_CONFIGS)
    out_spec = pl.BlockSpec((1, DIL_TILE, DIL_W), lambda b, i: (b, i, 0))
    return pl.pallas_call(
        functools.partial(_dil_kernel, dil=dil, group=group, seq_len=S),
        grid=(B, S // DIL_TILE),
        in_specs=[cur(g), prev(n_g + g), cur(n_g + g), nxt(n_g + g),
                  prev(2 * n_g + g), cur(2 * n_g + g), nxt(2 * n_g + g)],
        out_specs=[out_spec, out_spec],
        out_shape=[jax.ShapeDtypeStruct((B, S, DIL_W), F32), jax.ShapeDtypeStruct((B, S, DIL_W), F32)],
        scratch_shapes=[
            pltpu.VMEM((2, DIL_TILE, LANES), F32),
            pltpu.VMEM((2, DIL_TILE + 2 * halo, LANES), F32),
            pltpu.VMEM((2, DIL_TILE + 2 * halo, LANES), F32),
            pltpu.VMEM((2, DIL_TILE, LANES), F32),
            pltpu.VMEM((2, DIL_TILE, LANES), F32),
        ],
        compiler_params=_params(("parallel", "parallel")),
        name=f"dil_attn_d{dil}",
    )(zd, zd, zd, zd, zd, zd, zd)


def _out_kernel(x_ref, mod_ref, g1_ref, wg_ref, oa_ref, o0_ref, o1_ref, o2_ref, l0_ref, l1_ref, l2_ref,
                pa_ref, pb_ref, wo_ref, x1_ref):
    x = x_ref[0]
    mod = mod_ref[0]
    h = _rms(x, g1_ref[...]) * (1.0 + mod[1:2]) + mod[0:1]
    hb = h.astype(BF16)
    l0, l1, l2 = l0_ref[0], l1_ref[0], l2_ref[0]
    lm = jnp.maximum(jnp.maximum(l0, l1), l2)
    e0, e1, e2 = jnp.exp(l0 - lm), jnp.exp(l1 - lm), jnp.exp(l2 - lm)
    ob = (e0 * o0_ref[0] + e1 * o1_ref[0] + e2 * o2_ref[0]) / (e0 + e1 + e2)
    merged = jax.nn.sigmoid(_dot(hb, wg_ref[:, :D_MODEL])) * _dot(oa_ref[0], pa_ref[...])
    merged = merged + jax.nn.sigmoid(_dot(hb, wg_ref[:, D_MODEL:])) * _dot(ob.astype(BF16), pb_ref[...])
    x1_ref[0] = x + mod[2:3] * _dot(merged.astype(BF16), wo_ref[...])


def _out_call(x, mod, w, oa, dil_outs):
    B, S, _ = x.shape
    T = ROW_TILE
    row = lambda c: pl.BlockSpec((1, T, c), lambda b, i: (b, i, 0))
    os_ = [o for o, _ in dil_outs]
    ls_ = [l for _, l in dil_outs]
    return pl.pallas_call(
        _out_kernel,
        grid=(B, S // T),
        in_specs=[
            row(D_MODEL),
            pl.BlockSpec((1, 8, D_MODEL), lambda b, i: (b, 0, 0)),
            _const_spec((1, D_MODEL)),
            _const_spec((D_MODEL, 2 * D_MODEL)),
            row(MLA_HEADS * V_DIM),
            row(DIL_W), row(DIL_W), row(DIL_W), row(DIL_W), row(DIL_W), row(DIL_W),
            _const_spec((MLA_HEADS * V_DIM, D_MODEL)),
            _const_spec((DIL_W, D_MODEL)),
            _const_spec((D_MODEL, D_MODEL)),
        ],
        out_specs=row(D_MODEL),
        out_shape=jax.ShapeDtypeStruct((B, S, D_MODEL), F32),
        compiler_params=_params(("parallel", "parallel")),
        name="out_proj",
    )(x, mod, w["g1"], w["wg"], oa, *os_, *ls_, w["pa"], w["pb"], w["wo"])


def _ffn_kernel(xc_ref, xp_ref, xn_ref, mod_ref, g2_ref, wu_ref, cw_ref, cb_ref, wdn_ref, gf_ref, y_ref,
                h2s, a_s):
    T = xc_ref.shape[1]
    H = FFN_HALO
    mod = mod_ref[0]
    g2 = g2_ref[...]
    i = pl.program_id(1)
    n = pl.num_programs(1)

    def modnorm(v):
        return _rms(v, g2) * (1.0 + mod[4:5]) + mod[3:4]

    x1 = xc_ref[0]
    h2s[0:H] = jnp.where(i > 0, modnorm(xp_ref[0]), 0.0).astype(BF16)
    h2s[H:H + T] = modnorm(x1).astype(BF16)
    h2s[H + T:] = jnp.where(i < n - 1, modnorm(xn_ref[0]), 0.0).astype(BF16)

    rows = T + 2 * H
    for c in range(D_FF // FF_CHUNK):
        cs = slice(c * FF_CHUNK, (c + 1) * FF_CHUNK)
        ul = _dot(h2s[...], wu_ref[:, cs])
        gv = _dot(h2s[H:H + T], wu_ref[:, D_FF + c * FF_CHUNK:D_FF + (c + 1) * FF_CHUNK])
        u = (cb_ref[:, cs]
             + pltpu.roll(ul, 1, 0)[H:H + T] * cw_ref[0:1, cs]
             + ul[H:H + T] * cw_ref[1:2, cs]
             + pltpu.roll(ul, rows - 1, 0)[H:H + T] * cw_ref[2:3, cs])
        a_s[:, cs] = (jax.nn.gelu(u, approximate=True) * gv).astype(BF16)

    x2 = x1 + mod[5:6] * _dot(a_s[...], wdn_ref[...])
    y_ref[0] = _rms(x2, gf_ref[...])


def _ffn_call(x1, mod, w):
    B, S, _ = x1.shape
    T = ROW_TILE
    H = FFN_HALO
    per = T // H
    n_h = S // H
    return pl.pallas_call(
        _ffn_kernel,
        grid=(B, S // T),
        in_specs=[
            pl.BlockSpec((1, T, D_MODEL), lambda b, i: (b, i, 0)),
            pl.BlockSpec((1, H, D_MODEL), lambda b, i: (b, jnp.maximum(i * per - 1, 0), 0)),
            pl.BlockSpec((1, H, D_MODEL), lambda b, i: (b, jnp.minimum((i + 1) * per, n_h - 1), 0)),
            pl.BlockSpec((1, 8, D_MODEL), lambda b, i: (b, 0, 0)),
            _const_spec((1, D_MODEL)),
            _const_spec((D_MODEL, 2 * D_FF)),
            _const_spec((3, D_FF)),
            _const_spec((1, D_FF)),
            _const_spec((D_FF, D_MODEL)),
            _const_spec((1, D_MODEL)),
        ],
        out_specs=pl.BlockSpec((1, T, D_MODEL), lambda b, i: (b, i, 0)),
        out_shape=jax.ShapeDtypeStruct((B, S, D_MODEL), F32),
        scratch_shapes=[pltpu.VMEM((T + 2 * H, D_MODEL), BF16), pltpu.VMEM((T, D_FF), BF16)],
        compiler_params=_params(("parallel", "parallel")),
        name="conv_ffn",
    )(x1, x1, x1, mod, w["g2"], w["wu"], w["cw"], w["cb"], w["wdn"], w["gf"])


def _rope_tables(seq_len):
    inv = ROPE_THETA ** (-jnp.arange(0, QK_ROPE, 2, dtype=F32) / QK_ROPE)
    ang = jnp.arange(seq_len, dtype=F32)[:, None] * inv[None, :]
    cos, sin = jnp.cos(ang), jnp.sin(ang)
    zeros_l = jnp.zeros((seq_len, QK_NOPE), F32)
    zeros_r = jnp.zeros((seq_len, LANES - QK_NOPE - QK_ROPE), F32)
    ones_l = jnp.ones((seq_len, QK_NOPE), F32)
    scale = (QK_NOPE + QK_ROPE) ** -0.5 * math.log2(math.e)
    t1k = jnp.concatenate([zeros_l, cos, cos, zeros_r], axis=1)
    t2 = jnp.concatenate([zeros_l, -sin, sin, zeros_r], axis=1)
    t1q = jnp.concatenate([ones_l, cos, cos, zeros_r], axis=1) * scale
    return t1q.T, (t2 * scale).T, t1k, t2


def _prep_weights(norm1_g, w_in, q_norm_g, kv_norm_g, w_uq, w_ukv, p_a, p_b, w_out, norm2_g, w_up,
                  conv_w, conv_b, w_down, normf_g):
    half = QK_ROPE // 2
    c0 = Q_LORA + KV_LORA
    kr = w_in[:, c0:c0 + QK_ROPE]
    kr_chunk = jnp.concatenate(
        [jnp.zeros((D_MODEL, QK_NOPE), F32), kr, kr[:, half:], kr[:, :half]], axis=1)
    d0 = c0 + QK_ROPE
    wa = jnp.concatenate([w_in[:, :c0], kr_chunk], axis=1)
    wd = w_in[:, d0:d0 + ZD_COLS]
    wg = w_in[:, d0 + ZD_COLS:]
    uq = w_uq.reshape(Q_LORA, MLA_HEADS, QK_NOPE + QK_ROPE)
    rope = uq[:, :, QK_NOPE:]
    wq = jnp.concatenate([uq, rope[:, :, half:], rope[:, :, :half]], axis=2).reshape(Q_LORA, -1)
    ukv = w_ukv.reshape(KV_LORA, MLA_HEADS, QK_NOPE + V_DIM)
    wk = jnp.concatenate([ukv[:, :, :QK_NOPE], jnp.zeros((KV_LORA, MLA_HEADS, HEAD_PAD - QK_NOPE), F32)],
                         axis=2).reshape(KV_LORA, -1)
    wv = ukv[:, :, QK_NOPE:].reshape(KV_LORA, -1)
    b = lambda a: a.astype(BF16)
    return dict(
        g1=norm1_g.reshape(1, -1), wa=b(wa), wd=b(wd), wg=b(wg),
        qg=q_norm_g.reshape(1, -1), kvg=kv_norm_g.reshape(1, -1),
        wqt=b(wq.T), wk=b(wk), wvt=b(wv.T), pa=b(p_a), pb=b(p_b), wo=b(w_out),
        g2=norm2_g.reshape(1, -1), wu=b(w_up), cw=conv_w, cb=conv_b.reshape(1, -1), wdn=b(w_down),
        gf=normf_g.reshape(1, -1),
    )


def _trunk(x, mod, w):
    S = x.shape[1]
    assert S % DIL_TILE == 0 and S % ROW_TILE == 0 and S % MLA_TK == 0 and S % MLA_TQ == 0
    assert (S // MLA_TK) % 2 == 0
    qt, k, vt, zd = _in_call(x, mod, w, _rope_tables(S))
    oa = _mla_call(qt, k, vt)
    dil_outs = [_dil_call(zd, g, dil) for g, (_, dil) in enumerate(DIL_CONFIGS)]
    x1 = _out_call(x, mod, w, oa, dil_outs)
    return _ffn_call(x1, mod, w)


def _mods(c_list, ada_w, ada_b):
    c_all = jnp.concatenate(c_list, axis=0)
    n = c_all.shape[0]
    rows = -(-n // 8) * 8
    c_all = jnp.pad(c_all, ((0, rows - n), (0, 0)))
    mod = _mod_call(c_all, ada_w, ada_b.reshape(1, -1)).reshape(rows, 6, D_MODEL)
    mod = jnp.pad(mod, ((0, 0), (0, 2), (0, 0)))
    out, o = [], 0
    for c in c_list:
        out.append(mod[o:o + c.shape[0]])
        o += c.shape[0]
    return out


def kernel(x_prompt, x_sample, c_prompt, c_sample, ada_w, ada_b, norm1_g, w_in, q_norm_g, kv_norm_g, w_uq, w_ukv, p_a, p_b, w_out, norm2_g, w_up, conv_w, conv_b, w_down, normf_g):
    assert ada_w.shape[0] == 1, "single layer"
    w = _prep_weights(norm1_g[0], w_in[0], q_norm_g[0], kv_norm_g[0], w_uq[0], w_ukv[0], p_a[0], p_b[0],
                      w_out[0], norm2_g[0], w_up[0], conv_w[0], conv_b[0], w_down[0], normf_g)
    mod_p, mod_s = _mods([c_prompt, c_sample], ada_w[0], ada_b[0])
    return (_trunk(x_prompt, mod_p, w), _trunk(x_sample, mod_s, w))
```

```python
import functools
import math

import jax
import jax.numpy as jnp
import numpy as np
from jax import lax
from jax.experimental import pallas as pl
from jax.experimental.pallas import tpu as pltpu

F32 = jnp.float32
BF16 = jnp.bfloat16

D_MODEL = 1024
MLA_HEADS = 16
QK_NOPE = 64
QK_ROPE = 32
V_DIM = 64
Q_LORA = 384
KV_LORA = 256
ROPE_THETA = 10000.0
DIL_CONFIGS = ((128, 1), (512, 4), (2048, 16))
DIL_HEADS_PER_GROUP = 4
DIL_HEADS = DIL_HEADS_PER_GROUP * len(DIL_CONFIGS)
DIL_HEAD_DIM = 64
D_FF = 2816
EPS = 1e-6

LANES = 128
HEAD_PAD = 128
DIL_W = DIL_HEADS_PER_GROUP * DIL_HEAD_DIM
N_SIDE = 64
Q_BLK = 128
K_WIN = Q_BLK + 2 * N_SIDE
DIL_TILE = 2048
DIL_SPLIT = 4
ROW_TILE = 512
FFN_HALO = 16
FF_CHUNK = 256
MLA_TQ = 512
MLA_TK = 512
MLA_UNROLL = 8
MLA_LROWS = 16
NEG = -1e30
VMEM_LIMIT = 56 * 1024 * 1024

ZA_COLS = Q_LORA + KV_LORA + LANES
ZD_COLS = 3 * DIL_HEADS * DIL_HEAD_DIM


def _params(sem):
    return pltpu.CompilerParams(dimension_semantics=sem, vmem_limit_bytes=VMEM_LIMIT)


def _const_spec(shape):
    nd = len(shape)
    return pl.BlockSpec(shape, lambda *_: (0,) * nd, pipeline_mode=pl.Buffered(1))


def _rms(x, g):
    return x * lax.rsqrt(jnp.mean(x * x, axis=-1, keepdims=True) + EPS) * g


def _dot(a, b):
    return jnp.dot(a, b, preferred_element_type=F32)


def _dot_nt(a, b):
    return lax.dot_general(a, b, (((1,), (1,)), ((), ())), preferred_element_type=F32)


def _mod_kernel(c_ref, w_ref, b_ref, o_ref):
    c = c_ref[...]
    a = c * jax.nn.sigmoid(c)
    a_hi = a.astype(BF16)
    a_lo = (a - a_hi.astype(F32)).astype(BF16)
    w = w_ref[...]
    w_hi = w.astype(BF16)
    w_lo = (w - w_hi.astype(F32)).astype(BF16)
    o_ref[...] = _dot(a_hi, w_hi) + _dot(a_lo, w_hi) + _dot(a_hi, w_lo) + b_ref[...]


def _mod_call(c_all, ada_w, ada_b):
    rows = c_all.shape[0]
    n = ada_w.shape[1]
    tn = 1536
    return pl.pallas_call(
        _mod_kernel,
        grid=(n // tn,),
        in_specs=[
            pl.BlockSpec((rows, D_MODEL), lambda j: (0, 0)),
            pl.BlockSpec((D_MODEL, tn), lambda j: (0, j)),
            pl.BlockSpec((1, tn), lambda j: (0, j)),
        ],
        out_specs=pl.BlockSpec((rows, tn), lambda j: (0, j)),
        out_shape=jax.ShapeDtypeStruct((rows, n), F32),
        compiler_params=_params(("arbitrary",)),
        name="mod",
    )(c_all, ada_w, ada_b)


def _in_kernel(x_ref, mod_ref, g1_ref, wa_ref, wd_ref, qg_ref, kvg_ref, wqt_ref, wk_ref, wvt_ref,
               t1q_ref, t2q_ref, t1k_ref, t2k_ref, qt_ref, k_ref, vt_ref, zd_ref):
    x = x_ref[0]
    mod = mod_ref[0]
    h = _rms(x, g1_ref[...]) * (1.0 + mod[1:2]) + mod[0:1]
    hb = h.astype(BF16)
    zd_ref[0] = _dot(hb, wd_ref[...]).astype(BF16)
    za = _dot(hb, wa_ref[...])
    cq = _rms(za[:, :Q_LORA], qg_ref[...]).astype(BF16)
    ckv = _rms(za[:, Q_LORA:Q_LORA + KV_LORA], kvg_ref[...]).astype(BF16)
    kr = za[:, Q_LORA + KV_LORA:]
    kr = kr * t1k_ref[...] + pltpu.roll(kr, LANES - QK_ROPE, 1) * t2k_ref[...]
    vt_ref[0] = _dot_nt(wvt_ref[...], ckv).astype(BF16)
    qtf = _dot_nt(wqt_ref[...], cq)
    kf = _dot(ckv, wk_ref[...])
    t1q = t1q_ref[...]
    t2q = t2q_ref[...]
    for hd in range(MLA_HEADS):
        sl = slice(hd * HEAD_PAD, (hd + 1) * HEAD_PAD)
        qh = qtf[sl]
        qt_ref[0, sl, :] = (qh * t1q + pltpu.roll(qh, HEAD_PAD - QK_ROPE, 0) * t2q).astype(BF16)
        k_ref[0, :, sl] = (kf[:, sl] + kr).astype(BF16)


def _in_call(x, mod, w, tabs):
    B, S, _ = x.shape
    T = ROW_TILE
    row = lambda c: pl.BlockSpec((1, T, c), lambda b, i: (b, i, 0))
    tab = pl.BlockSpec((T, LANES), lambda b, i: (i, 0))
    tab_t = pl.BlockSpec((LANES, T), lambda b, i: (0, i))
    col = lambda r: pl.BlockSpec((1, r, T), lambda b, i: (b, 0, i))
    qk_cols = MLA_HEADS * HEAD_PAD
    return pl.pallas_call(
        _in_kernel,
        grid=(B, S // T),
        in_specs=[
            row(D_MODEL),
            pl.BlockSpec((1, 8, D_MODEL), lambda b, i: (b, 0, 0)),
            _const_spec((1, D_MODEL)),
            _const_spec((D_MODEL, ZA_COLS)),
            _const_spec((D_MODEL, ZD_COLS)),
            _const_spec((1, Q_LORA)),
            _const_spec((1, KV_LORA)),
            _const_spec((qk_cols, Q_LORA)),
            _const_spec((KV_LORA, qk_cols)),
            _const_spec((MLA_HEADS * V_DIM, KV_LORA)),
            tab_t, tab_t, tab, tab,
        ],
        out_specs=[col(qk_cols), row(qk_cols), col(MLA_HEADS * V_DIM), row(ZD_COLS)],
        out_shape=[
            jax.ShapeDtypeStruct((B, qk_cols, S), BF16),
            jax.ShapeDtypeStruct((B, S, qk_cols), BF16),
            jax.ShapeDtypeStruct((B, MLA_HEADS * V_DIM, S), BF16),
            jax.ShapeDtypeStruct((B, S, ZD_COLS), BF16),
        ],
        compiler_params=_params(("parallel", "parallel")),
        name="in_proj",
    )(x, mod, w["g1"], w["wa"], w["wd"], w["qg"], w["kvg"], w["wqt"], w["wk"], w["wvt"], *tabs)


def _mla_kernel(qt_ref, k_ref, vt_ref, o_ref, s0_ref, s1_ref, acc_ref, *, n_q, n_kv, unroll):
    tq, tk = MLA_TQ, MLA_TK
    ones = jnp.ones((MLA_LROWS, tk), BF16)
    slots = (s0_ref, s1_ref)
    n_chunks = n_q * n_kv

    def q_cols(c):
        return pl.ds(pl.multiple_of((c // n_kv) * tq, tq), tq)

    def kv_rows(c):
        return pl.ds(pl.multiple_of((c % n_kv) * tk, tk), tk)

    def scores(c, slot):
        mx = []
        for hh in range(2):
            sl = slice(hh * HEAD_PAD, (hh + 1) * HEAD_PAD)
            st = _dot(k_ref[0, kv_rows(c), sl], qt_ref[0, sl, q_cols(c)])
            slots[slot][hh] = st
            mx.append(jnp.max(st, axis=0, keepdims=True))
        return tuple(mx)

    def accumulate(c, slot, m, mx):
        first = (c % n_kv) == 0
        out = []
        for hh in range(2):
            m_old = jnp.where(first, -jnp.inf, m[hh])
            m_new = jnp.maximum(m_old, mx[hh])
            alpha = jnp.exp2(m_old - m_new)
            pt = jnp.exp2(slots[slot][hh] - m_new).astype(BF16)
            lhs = jnp.concatenate([vt_ref[0, hh * V_DIM:(hh + 1) * V_DIM, kv_rows(c)], ones], axis=0)
            acc_ref[hh] = alpha * acc_ref[hh] + _dot(lhs, pt)
            out.append(m_new)
        return tuple(out)

    def finalize(c):
        ot = jnp.concatenate([acc_ref[hh, :V_DIM] / acc_ref[hh, V_DIM:V_DIM + 1] for hh in range(2)], axis=0)
        o_ref[0, q_cols(c), :] = ot.T.astype(BF16)

    def run(c0, mx, m, prefetch_last):
        for t in range(unroll):
            if t < unroll - 1 or prefetch_last:
                mx_next = scores(c0 + t + 1, (t + 1) % 2)
            m = accumulate(c0 + t, t % 2, m, mx)
            mx = mx_next
        return m, mx

    acc_ref[...] = jnp.zeros_like(acc_ref)
    m = tuple(jnp.full((1, tq), -jnp.inf, F32) for _ in range(2))
    mx = scores(0, 0)

    def body(i, carry):
        c0 = unroll * i
        m, mx = run(c0, carry[1], carry[0], True)
        last = c0 + unroll - 1

        @pl.when(last % n_kv == n_kv - 1)
        def _():
            finalize(last)

        return m, mx

    m, mx = lax.fori_loop(0, n_chunks // unroll - 1, body, (m, mx))
    run(n_chunks - unroll, mx, m, False)
    finalize(n_chunks - 1)


def _mla_call(qt, k, vt):
    B, S, _ = k.shape
    tq, tk = MLA_TQ, MLA_TK
    return pl.pallas_call(
        functools.partial(_mla_kernel, n_q=S // tq, n_kv=S // tk, unroll=math.gcd(MLA_UNROLL, S // tk)),
        grid=(B, MLA_HEADS // 2),
        in_specs=[
            pl.BlockSpec((1, 2 * HEAD_PAD, S), lambda b, h: (b, h, 0)),
            pl.BlockSpec((1, S, 2 * HEAD_PAD), lambda b, h: (b, 0, h)),
            pl.BlockSpec((1, 2 * V_DIM, S), lambda b, h: (b, h, 0)),
        ],
        out_specs=pl.BlockSpec((1, S, 2 * V_DIM), lambda b, h: (b, 0, h)),
        out_shape=jax.ShapeDtypeStruct((B, S, MLA_HEADS * V_DIM), BF16),
        scratch_shapes=[
            pltpu.VMEM((2, tk, tq), F32),
            pltpu.VMEM((2, tk, tq), F32),
            pltpu.VMEM((2, V_DIM + MLA_LROWS, tq), F32),
        ],
        compiler_params=_params(("parallel", "parallel")),
        name="mla_attn",
    )(qt, k, vt)


def _dil_kernel(q_ref, kp_ref, kc_ref, kn_ref, vp_ref, vc_ref, vn_ref, o_ref, lse_ref,
                qs, ks, vs, os_, ls_, *split, dil, group, seq_len):
    halo = N_SIDE * dil
    n_blk = DIL_TILE // (Q_BLK * dil)
    sc_len = seq_len // dil
    i0 = pl.program_id(1) * (DIL_TILE // dil)

    for pair in range(2):
        cs = slice(pair * LANES, (pair + 1) * LANES)
        qs[pair] = q_ref[0, :, cs].astype(F32)
        for dst, (p_ref, c_ref, n_ref) in ((ks, (kp_ref, kc_ref, kn_ref)), (vs, (vp_ref, vc_ref, vn_ref))):
            dst[pair, 0:halo] = p_ref[0, :, cs].astype(F32)
            dst[pair, halo:halo + DIL_TILE] = c_ref[0, :, cs].astype(F32)
            dst[pair, halo + DIL_TILE:] = n_ref[0, :, cs].astype(F32)

    step = dil // DIL_SPLIT if split else dil
    if split:
        for src, dst in zip((qs, ks, vs), split):
            for pair in range(2):
                for a in range(DIL_SPLIT):
                    dst[pair, a] = src[pair, pl.ds(a, src.shape[1] // DIL_SPLIT, stride=DIL_SPLIT), :]

    def strided(src_idx, pair, r, first, n):
        if split:
            a, b = r % DIL_SPLIT, r // DIL_SPLIT
            return split[src_idx][pair, a, pl.ds(first * step + b, n, stride=step), :]
        src = (qs, ks, vs)[src_idx]
        rows = pl.ds(first * dil + r, n, stride=dil) if dil > 1 else pl.ds(first + r, n)
        return src[pair, rows, :]

    col = lax.broadcasted_iota(jnp.int32, (Q_BLK, K_WIN), 1)
    rowi = lax.broadcasted_iota(jnp.int32, (Q_BLK, K_WIN), 0)
    rel = col - N_SIDE - rowi
    absrel = jnp.abs(rel).astype(F32)
    band = jnp.abs(rel) <= N_SIDE
    lane = lax.broadcasted_iota(jnp.int32, (Q_BLK, LANES), 1)
    first_head = lane < DIL_HEAD_DIM
    scale = DIL_HEAD_DIM ** -0.5
    slopes = [2.0 ** (-8.0 * (group * DIL_HEADS_PER_GROUP + h + 1) / DIL_HEADS) for h in range(DIL_HEADS_PER_GROUP)]
    bias = [jnp.where(band, -(absrel * (slope * dil)), NEG) for slope in slopes]

    for jb in range(n_blk):
        kc0 = i0 + (jb * Q_BLK - N_SIDE)
        in_seq = (col >= -kc0) & (col < sc_len - kc0) if jb in (0, n_blk - 1) else None
        for r in range(dil):
            start = jb * Q_BLK * dil + r
            q_rows = pl.ds(start, Q_BLK, stride=dil) if dil > 1 else pl.ds(start, Q_BLK)
            for pair in range(2):
                qp = strided(0, pair, r, jb * Q_BLK, Q_BLK) * scale
                kp = strided(1, pair, r, jb * Q_BLK, K_WIN).astype(BF16)
                vp = strided(2, pair, r, jb * Q_BLK, K_WIN).astype(BF16)
                o_acc = None
                lse_acc = None
                for hh in range(2):
                    sel = first_head if hh == 0 else jnp.logical_not(first_head)
                    qm = jnp.where(sel, qp, 0.0).astype(BF16)
                    s = _dot_nt(qm, kp) + bias[pair * 2 + hh]
                    if in_seq is not None:
                        s = jnp.where(in_seq, s, NEG)
                    m = jnp.max(s, axis=1, keepdims=True)
                    p = jnp.exp(s - m)
                    l = jnp.sum(p, axis=1, keepdims=True)
                    o = _dot(p.astype(BF16), vp) * (1.0 / l)
                    lse = jnp.broadcast_to(m + jnp.log(l), (Q_BLK, LANES))
                    o_acc = o if hh == 0 else jnp.where(sel, o, o_acc)
                    lse_acc = lse if hh == 0 else jnp.where(sel, lse, lse_acc)
                os_[pair, q_rows, :] = o_acc
                ls_[pair, q_rows, :] = lse_acc

    for pair in range(2):
        cs = slice(pair * LANES, (pair + 1) * LANES)
        o_ref[0, :, cs] = os_[pair]
        lse_ref[0, :, cs] = ls_[pair]


def _dil_call(zd, group, dil):
    B, S, _ = zd.shape
    halo = N_SIDE * dil
    per = DIL_TILE // halo
    n_halo = S // halo
    g = group
    cur = lambda c: pl.BlockSpec((1, DIL_TILE, DIL_W), lambda b, i: (b, i, c))
    prev = lambda c: pl.BlockSpec((1, halo, DIL_W), lambda b, i: (b, jnp.maximum(i * per - 1, 0), c))
    nxt = lambda c: pl.BlockSpec((1, halo, DIL_W), lambda b, i: (b, jnp.minimum((i + 1) * per, n_halo - 1), c))
    n_g = len(DIL_CONFIGS)
    out_spec = pl.BlockSpec((1, DIL_TILE, DIL_W), lambda b, i: (b, i, 0))
    return pl.pallas_call(
        functools.partial(_dil_kernel, dil=dil, group=group, seq_len=S),
        grid=(B, S // DIL_TILE),
        in_specs=[cur(g), prev(n_g + g), cur(n_g + g), nxt(n_g + g),
                  prev(2 * n_g + g), cur(2 * n_g + g), nxt(2 * n_g + g)],
        out_specs=[out_spec, out_spec],
        out_shape=[jax.ShapeDtypeStruct((B, S, DIL_W), F32), jax.ShapeDtypeStruct((B, S, DIL_W), F32)],
        scratch_shapes=[
            pltpu.VMEM((2, DIL_TILE, LANES), F32),
            pltpu.VMEM((2, DIL_TILE + 2 * halo, LANES), F32),
            pltpu.VMEM((2, DIL_TILE + 2 * halo, LANES), F32),
            pltpu.VMEM((2, DIL_TILE, LANES), F32),
            pltpu.VMEM((2, DIL_TILE, LANES), F32),
        ] + ([pltpu.VMEM((2, DIL_SPLIT, rows // DIL_SPLIT, LANES), F32)
              for rows in (DIL_TILE, DIL_TILE + 2 * halo, DIL_TILE + 2 * halo)] if dil > DIL_SPLIT else []),
        compiler_params=_params(("parallel", "parallel")),
        name=f"dil_attn_d{dil}",
    )(zd, zd, zd, zd, zd, zd, zd)


def _out_kernel(x_ref, mod_ref, g1_ref, wg_ref, oa_ref, o0_ref, o1_ref, o2_ref, l0_ref, l1_ref, l2_ref,
                pa_ref, pb_ref, wo_ref, x1_ref):
    x = x_ref[0]
    mod = mod_ref[0]
    h = _rms(x, g1_ref[...]) * (1.0 + mod[1:2]) + mod[0:1]
    hb = h.astype(BF16)
    l0, l1, l2 = l0_ref[0], l1_ref[0], l2_ref[0]
    lm = jnp.maximum(jnp.maximum(l0, l1), l2)
    e0, e1, e2 = jnp.exp(l0 - lm), jnp.exp(l1 - lm), jnp.exp(l2 - lm)
    ob = (e0 * o0_ref[0] + e1 * o1_ref[0] + e2 * o2_ref[0]) / (e0 + e1 + e2)
    merged = jax.nn.sigmoid(_dot(hb, wg_ref[:, :D_MODEL])) * _dot(oa_ref[0], pa_ref[...])
    merged = merged + jax.nn.sigmoid(_dot(hb, wg_ref[:, D_MODEL:])) * _dot(ob.astype(BF16), pb_ref[...])
    x1_ref[0] = x + mod[2:3] * _dot(merged.astype(BF16), wo_ref[...])


def _out_call(x, mod, w, oa, dil_outs):
    B, S, _ = x.shape
    T = ROW_TILE
    row = lambda c: pl.BlockSpec((1, T, c), lambda b, i: (b, i, 0))
    os_ = [o for o, _ in dil_outs]
    ls_ = [l for _, l in dil_outs]
    return pl.pallas_call(
        _out_kernel,
        grid=(B, S // T),
        in_specs=[
            row(D_MODEL),
            pl.BlockSpec((1, 8, D_MODEL), lambda b, i: (b, 0, 0)),
            _const_spec((1, D_MODEL)),
            _const_spec((D_MODEL, 2 * D_MODEL)),
            row(MLA_HEADS * V_DIM),
            row(DIL_W), row(DIL_W), row(DIL_W), row(DIL_W), row(DIL_W), row(DIL_W),
            _const_spec((MLA_HEADS * V_DIM, D_MODEL)),
            _const_spec((DIL_W, D_MODEL)),
            _const_spec((D_MODEL, D_MODEL)),
        ],
        out_specs=row(D_MODEL),
        out_shape=jax.ShapeDtypeStruct((B, S, D_MODEL), F32),
        compiler_params=_params(("parallel", "parallel")),
        name="out_proj",
    )(x, mod, w["g1"], w["wg"], oa, *os_, *ls_, w["pa"], w["pb"], w["wo"])


def _ffn_kernel(xc_ref, xp_ref, xn_ref, mod_ref, g2_ref, wu_ref, cw_ref, cb_ref, wdn_ref, gf_ref, y_ref,
                h2s, a_s):
    T = xc_ref.shape[1]
    H = FFN_HALO
    mod = mod_ref[0]
    g2 = g2_ref[...]
    i = pl.program_id(1)
    n = pl.num_programs(1)

    def modnorm(v):
        return _rms(v, g2) * (1.0 + mod[4:5]) + mod[3:4]

    x1 = xc_ref[0]
    h2s[0:H] = jnp.where(i > 0, modnorm(xp_ref[0]), 0.0).astype(BF16)
    h2s[H:H + T] = modnorm(x1).astype(BF16)
    h2s[H + T:] = jnp.where(i < n - 1, modnorm(xn_ref[0]), 0.0).astype(BF16)

    rows = T + 2 * H
    for c in range(D_FF // FF_CHUNK):
        cs = slice(c * FF_CHUNK, (c + 1) * FF_CHUNK)
        ul = _dot(h2s[...], wu_ref[:, cs])
        gv = _dot(h2s[H:H + T], wu_ref[:, D_FF + c * FF_CHUNK:D_FF + (c + 1) * FF_CHUNK])
        u = (cb_ref[:, cs]
             + pltpu.roll(ul, 1, 0)[H:H + T] * cw_ref[0:1, cs]
             + ul[H:H + T] * cw_ref[1:2, cs]
             + pltpu.roll(ul, rows - 1, 0)[H:H + T] * cw_ref[2:3, cs])
        a_s[:, cs] = (jax.nn.gelu(u, approximate=True) * gv).astype(BF16)

    x2 = x1 + mod[5:6] * _dot(a_s[...], wdn_ref[...])
    y_ref[0] = _rms(x2, gf_ref[...])


def _ffn_call(x1, mod, w):
    B, S, _ = x1.shape
    T = ROW_TILE
    H = FFN_HALO
    per = T // H
    n_h = S // H
    return pl.pallas_call(
        _ffn_kernel,
        grid=(B, S // T),
        in_specs=[
            pl.BlockSpec((1, T, D_MODEL), lambda b, i: (b, i, 0)),
            pl.BlockSpec((1, H, D_MODEL), lambda b, i: (b, jnp.maximum(i * per - 1, 0), 0)),
            pl.BlockSpec((1, H, D_MODEL), lambda b, i: (b, jnp.minimum((i + 1) * per, n_h - 1), 0)),
            pl.BlockSpec((1, 8, D_MODEL), lambda b, i: (b, 0, 0)),
            _const_spec((1, D_MODEL)),
            _const_spec((D_MODEL, 2 * D_FF)),
            _const_spec((3, D_FF)),
            _const_spec((1, D_FF)),
            _const_spec((D_FF, D_MODEL)),
            _const_spec((1, D_MODEL)),
        ],
        out_specs=pl.BlockSpec((1, T, D_MODEL), lambda b, i: (b, i, 0)),
        out_shape=jax.ShapeDtypeStruct((B, S, D_MODEL), F32),
        scratch_shapes=[pltpu.VMEM((T + 2 * H, D_MODEL), BF16), pltpu.VMEM((T, D_FF), BF16)],
        compiler_params=_params(("parallel", "parallel")),
        name="conv_ffn",
    )(x1, x1, x1, mod, w["g2"], w["wu"], w["cw"], w["cb"], w["wdn"], w["gf"])


def _rope_tables(seq_len):
    inv = ROPE_THETA ** (-jnp.arange(0, QK_ROPE, 2, dtype=F32) / QK_ROPE)
    ang = jnp.arange(seq_len, dtype=F32)[:, None] * inv[None, :]
    cos, sin = jnp.cos(ang), jnp.sin(ang)
    zeros_l = jnp.zeros((seq_len, QK_NOPE), F32)
    zeros_r = jnp.zeros((seq_len, LANES - QK_NOPE - QK_ROPE), F32)
    ones_l = jnp.ones((seq_len, QK_NOPE), F32)
    scale = (QK_NOPE + QK_ROPE) ** -0.5 * math.log2(math.e)
    t1k = jnp.concatenate([zeros_l, cos, cos, zeros_r], axis=1)
    t2 = jnp.concatenate([zeros_l, -sin, sin, zeros_r], axis=1)
    t1q = jnp.concatenate([ones_l, cos, cos, zeros_r], axis=1) * scale
    return t1q.T, (t2 * scale).T, t1k, t2


def _prep_weights(norm1_g, w_in, q_norm_g, kv_norm_g, w_uq, w_ukv, p_a, p_b, w_out, norm2_g, w_up,
                  conv_w, conv_b, w_down, normf_g):
    half = QK_ROPE // 2
    c0 = Q_LORA + KV_LORA
    kr = w_in[:, c0:c0 + QK_ROPE]
    kr_chunk = jnp.concatenate(
        [jnp.zeros((D_MODEL, QK_NOPE), F32), kr, kr[:, half:], kr[:, :half]], axis=1)
    d0 = c0 + QK_ROPE
    wa = jnp.concatenate([w_in[:, :c0], kr_chunk], axis=1)
    wd = w_in[:, d0:d0 + ZD_COLS]
    wg = w_in[:, d0 + ZD_COLS:]
    uq = w_uq.reshape(Q_LORA, MLA_HEADS, QK_NOPE + QK_ROPE)
    rope = uq[:, :, QK_NOPE:]
    wq = jnp.concatenate([uq, rope[:, :, half:], rope[:, :, :half]], axis=2).reshape(Q_LORA, -1)
    ukv = w_ukv.reshape(KV_LORA, MLA_HEADS, QK_NOPE + V_DIM)
    wk = jnp.concatenate([ukv[:, :, :QK_NOPE], jnp.zeros((KV_LORA, MLA_HEADS, HEAD_PAD - QK_NOPE), F32)],
                         axis=2).reshape(KV_LORA, -1)
    wv = ukv[:, :, QK_NOPE:].reshape(KV_LORA, -1)
    b = lambda a: a.astype(BF16)
    return dict(
        g1=norm1_g.reshape(1, -1), wa=b(wa), wd=b(wd), wg=b(wg),
        qg=q_norm_g.reshape(1, -1), kvg=kv_norm_g.reshape(1, -1),
        wqt=b(wq.T), wk=b(wk), wvt=b(wv.T), pa=b(p_a), pb=b(p_b), wo=b(w_out),
        g2=norm2_g.reshape(1, -1), wu=b(w_up), cw=conv_w, cb=conv_b.reshape(1, -1), wdn=b(w_down),
        gf=normf_g.reshape(1, -1),
    )


def _trunk(x, mod, w):
    S = x.shape[1]
    assert S % DIL_TILE == 0 and S % ROW_TILE == 0 and S % MLA_TK == 0 and S % MLA_TQ == 0
    assert (S // MLA_TK) % 2 == 0
    qt, k, vt, zd = _in_call(x, mod, w, _rope_tables(S))
    oa = _mla_call(qt, k, vt)
    dil_outs = [_dil_call(zd, g, dil) for g, (_, dil) in enumerate(DIL_CONFIGS)]
    x1 = _out_call(x, mod, w, oa, dil_outs)
    return _ffn_call(x1, mod, w)


def _mods(c_list, ada_w, ada_b):
    c_all = jnp.concatenate(c_list, axis=0)
    n = c_all.shape[0]
    rows = -(-n // 8) * 8
    c_all = jnp.pad(c_all, ((0, rows - n), (0, 0)))
    mod = _mod_call(c_all, ada_w, ada_b.reshape(1, -1)).reshape(rows, 6, D_MODEL)
    mod = jnp.pad(mod, ((0, 0), (0, 2), (0, 0)))
    out, o = [], 0
    for c in c_list:
        out.append(mod[o:o + c.shape[0]])
        o += c.shape[0]
    return out


def kernel(x_prompt, x_sample, c_prompt, c_sample, ada_w, ada_b, norm1_g, w_in, q_norm_g, kv_norm_g, w_uq, w_ukv, p_a, p_b, w_out, norm2_g, w_up, conv_w, conv_b, w_down, normf_g):
    assert ada_w.shape[0] == 1, "single layer"
    w = _prep_weights(norm1_g[0], w_in[0], q_norm_g[0], kv_norm_g[0], w_uq[0], w_ukv[0], p_a[0], p_b[0],
                      w_out[0], norm2_g[0], w_up[0], conv_w[0], conv_b[0], w_down[0], normf_g)
    mod_p, mod_s = _mods([c_prompt, c_sample], ada_w[0], ada_b[0])
    return (_trunk(x_prompt, mod_p, w), _trunk(x_sample, mod_s, w))
```

```python
import functools
import math

import jax
import jax.numpy as jnp
import numpy as np
from jax import lax
from jax.experimental import pallas as pl
from jax.experimental.pallas import tpu as pltpu

F32 = jnp.float32
BF16 = jnp.bfloat16

D_MODEL = 1024
MLA_HEADS = 16
QK_NOPE = 64
QK_ROPE = 32
V_DIM = 64
Q_LORA = 384
KV_LORA = 256
ROPE_THETA = 10000.0
DIL_CONFIGS = ((128, 1), (512, 4), (2048, 16))
DIL_HEADS_PER_GROUP = 4
DIL_HEADS = DIL_HEADS_PER_GROUP * len(DIL_CONFIGS)
DIL_HEAD_DIM = 64
D_FF = 2816
EPS = 1e-6

LANES = 128
HEAD_PAD = 128
DIL_W = DIL_HEADS_PER_GROUP * DIL_HEAD_DIM
N_SIDE = 64
Q_BLK = 128
K_WIN = Q_BLK + 2 * N_SIDE
DIL_TILE = 2048
DIL_SPLIT = 4
ROW_TILE = 512
FFN_HALO = 16
FF_CHUNK = 256
MLA_TQ = 512
MLA_TK = 512
MXU_ROWS = 256
MLA_UNROLL = 8
MLA_LROWS = 16
NEG = -1e30
VMEM_LIMIT = 56 * 1024 * 1024

ZA_COLS = Q_LORA + KV_LORA + LANES
ZD_COLS = 3 * DIL_HEADS * DIL_HEAD_DIM


def _params(sem):
    return pltpu.CompilerParams(dimension_semantics=sem, vmem_limit_bytes=VMEM_LIMIT)


def _const_spec(shape):
    nd = len(shape)
    return pl.BlockSpec(shape, lambda *_: (0,) * nd, pipeline_mode=pl.Buffered(1))


def _rms(x, g):
    return x * lax.rsqrt(jnp.mean(x * x, axis=-1, keepdims=True) + EPS) * g


def _dot(a, b):
    return jnp.dot(a, b, preferred_element_type=F32)


def _dot_nt(a, b):
    return lax.dot_general(a, b, (((1,), (1,)), ((), ())), preferred_element_type=F32)


def _mod_kernel(c_ref, w_ref, b_ref, o_ref):
    c = c_ref[...]
    a = c * jax.nn.sigmoid(c)
    a_hi = a.astype(BF16)
    a_lo = (a - a_hi.astype(F32)).astype(BF16)
    w = w_ref[...]
    w_hi = w.astype(BF16)
    w_lo = (w - w_hi.astype(F32)).astype(BF16)
    o_ref[...] = _dot(a_hi, w_hi) + _dot(a_lo, w_hi) + _dot(a_hi, w_lo) + b_ref[...]


def _mod_call(c_all, ada_w, ada_b):
    rows = c_all.shape[0]
    n = ada_w.shape[1]
    tn = 1536
    return pl.pallas_call(
        _mod_kernel,
        grid=(n // tn,),
        in_specs=[
            pl.BlockSpec((rows, D_MODEL), lambda j: (0, 0)),
            pl.BlockSpec((D_MODEL, tn), lambda j: (0, j)),
            pl.BlockSpec((1, tn), lambda j: (0, j)),
        ],
        out_specs=pl.BlockSpec((rows, tn), lambda j: (0, j)),
        out_shape=jax.ShapeDtypeStruct((rows, n), F32),
        compiler_params=_params(("arbitrary",)),
        name="mod",
    )(c_all, ada_w, ada_b)


def _in_kernel(x_ref, mod_ref, g1_ref, wa_ref, wd_ref, qg_ref, kvg_ref, wqt_ref, wk_ref, wvt_ref,
               t1q_ref, t2q_ref, t1k_ref, t2k_ref, qt_ref, k_ref, vt_ref, zd_ref):
    x = x_ref[0]
    mod = mod_ref[0]
    h = _rms(x, g1_ref[...]) * (1.0 + mod[1:2]) + mod[0:1]
    hb = h.astype(BF16)
    zd_ref[0] = _dot(hb, wd_ref[...]).astype(BF16)
    za = _dot(hb, wa_ref[...])
    cq = _rms(za[:, :Q_LORA], qg_ref[...]).astype(BF16)
    ckv = _rms(za[:, Q_LORA:Q_LORA + KV_LORA], kvg_ref[...]).astype(BF16)
    kr = za[:, Q_LORA + KV_LORA:]
    kr = kr * t1k_ref[...] + pltpu.roll(kr, LANES - QK_ROPE, 1) * t2k_ref[...]
    vt_ref[0] = _dot_nt(wvt_ref[...], ckv).astype(BF16)
    qtf = _dot_nt(wqt_ref[...], cq)
    kf = _dot(ckv, wk_ref[...])
    t1q = t1q_ref[...]
    t2q = t2q_ref[...]
    for hd in range(MLA_HEADS):
        sl = slice(hd * HEAD_PAD, (hd + 1) * HEAD_PAD)
        qh = qtf[sl]
        qt_ref[0, sl, :] = (qh * t1q + pltpu.roll(qh, HEAD_PAD - QK_ROPE, 0) * t2q).astype(BF16)
        k_ref[0, :, sl] = (kf[:, sl] + kr).astype(BF16)


def _in_call(x, mod, w, tabs):
    B, S, _ = x.shape
    T = ROW_TILE
    row = lambda c: pl.BlockSpec((1, T, c), lambda b, i: (b, i, 0))
    tab = pl.BlockSpec((T, LANES), lambda b, i: (i, 0))
    tab_t = pl.BlockSpec((LANES, T), lambda b, i: (0, i))
    col = lambda r: pl.BlockSpec((1, r, T), lambda b, i: (b, 0, i))
    qk_cols = MLA_HEADS * HEAD_PAD
    return pl.pallas_call(
        _in_kernel,
        grid=(B, S // T),
        in_specs=[
            row(D_MODEL),
            pl.BlockSpec((1, 8, D_MODEL), lambda b, i: (b, 0, 0)),
            _const_spec((1, D_MODEL)),
            _const_spec((D_MODEL, ZA_COLS)),
            _const_spec((D_MODEL, ZD_COLS)),
            _const_spec((1, Q_LORA)),
            _const_spec((1, KV_LORA)),
            _const_spec((qk_cols, Q_LORA)),
            _const_spec((KV_LORA, qk_cols)),
            _const_spec((MLA_HEADS * V_DIM, KV_LORA)),
            tab_t, tab_t, tab, tab,
        ],
        out_specs=[col(qk_cols), row(qk_cols), col(MLA_HEADS * V_DIM), row(ZD_COLS)],
        out_shape=[
            jax.ShapeDtypeStruct((B, qk_cols, S), BF16),
            jax.ShapeDtypeStruct((B, S, qk_cols), BF16),
            jax.ShapeDtypeStruct((B, MLA_HEADS * V_DIM, S), BF16),
            jax.ShapeDtypeStruct((B, S, ZD_COLS), BF16),
        ],
        compiler_params=_params(("parallel", "parallel")),
        name="in_proj",
    )(x, mod, w["g1"], w["wa"], w["wd"], w["qg"], w["kvg"], w["wqt"], w["wk"], w["wvt"], *tabs)


def _mla_kernel(qt_ref, k_ref, vt_ref, o_ref, s0_ref, s1_ref, acc_ref, *, n_q, n_kv, unroll):
    tq, tk = s0_ref.shape[2], s0_ref.shape[1]
    ones = jnp.ones((MLA_LROWS, tk), BF16)
    slots = (s0_ref, s1_ref)
    n_chunks = n_q * n_kv

    def q_cols(c):
        return pl.ds(pl.multiple_of((c // n_kv) * tq, tq), tq)

    def kv_rows(c):
        return pl.ds(pl.multiple_of((c % n_kv) * tk, tk), tk)

    def scores(c, slot):
        mx = []
        for hh in range(2):
            sl = slice(hh * HEAD_PAD, (hh + 1) * HEAD_PAD)
            st = _dot(k_ref[0, kv_rows(c), sl], qt_ref[0, sl, q_cols(c)])
            slots[slot][hh] = st
            mx.append(jnp.max(st, axis=0, keepdims=True))
        return tuple(mx)

    def accumulate(c, slot, m, mx):
        first = (c % n_kv) == 0
        out = []
        for hh in range(2):
            m_old = jnp.where(first, -jnp.inf, m[hh])
            m_new = jnp.maximum(m_old, mx[hh])
            alpha = jnp.exp2(m_old - m_new)
            pt = jnp.exp2(slots[slot][hh] - m_new).astype(BF16)
            lhs = jnp.concatenate([vt_ref[0, hh * V_DIM:(hh + 1) * V_DIM, kv_rows(c)], ones], axis=0)
            acc_ref[hh] = alpha * acc_ref[hh] + _dot(lhs, pt)
            out.append(m_new)
        return tuple(out)

    def finalize(c):
        ot = jnp.concatenate([acc_ref[hh, :V_DIM] / acc_ref[hh, V_DIM:V_DIM + 1] for hh in range(2)], axis=0)
        o_ref[0, q_cols(c), :] = ot.T.astype(BF16)

    def run(c0, mx, m, prefetch_last):
        for t in range(unroll):
            if t < unroll - 1 or prefetch_last:
                mx_next = scores(c0 + t + 1, (t + 1) % 2)
            m = accumulate(c0 + t, t % 2, m, mx)
            mx = mx_next
        return m, mx

    acc_ref[...] = jnp.zeros_like(acc_ref)
    m = tuple(jnp.full((1, tq), -jnp.inf, F32) for _ in range(2))
    mx = scores(0, 0)

    def body(i, carry):
        c0 = unroll * i
        m, mx = run(c0, carry[1], carry[0], True)
        last = c0 + unroll - 1

        @pl.when(last % n_kv == n_kv - 1)
        def _():
            finalize(last)

        return m, mx

    m, mx = lax.fori_loop(0, n_chunks // unroll - 1, body, (m, mx))
    run(n_chunks - unroll, mx, m, False)
    finalize(n_chunks - 1)


def _mla_call(qt, k, vt):
    B, S, _ = k.shape
    tq, tk = MLA_TQ, min(MLA_TK, S // MLA_UNROLL)
    assert S % tk == 0 and (S // tk) % 2 == 0 and tk % MXU_ROWS == 0
    return pl.pallas_call(
        functools.partial(_mla_kernel, n_q=S // tq, n_kv=S // tk, unroll=math.gcd(MLA_UNROLL, S // tk)),
        grid=(B, MLA_HEADS // 2),
        in_specs=[
            pl.BlockSpec((1, 2 * HEAD_PAD, S), lambda b, h: (b, h, 0)),
            pl.BlockSpec((1, S, 2 * HEAD_PAD), lambda b, h: (b, 0, h)),
            pl.BlockSpec((1, 2 * V_DIM, S), lambda b, h: (b, h, 0)),
        ],
        out_specs=pl.BlockSpec((1, S, 2 * V_DIM), lambda b, h: (b, 0, h)),
        out_shape=jax.ShapeDtypeStruct((B, S, MLA_HEADS * V_DIM), BF16),
        scratch_shapes=[
            pltpu.VMEM((2, tk, tq), F32),
            pltpu.VMEM((2, tk, tq), F32),
            pltpu.VMEM((2, V_DIM + MLA_LROWS, tq), F32),
        ],
        compiler_params=_params(("parallel", "parallel")),
        name="mla_attn",
    )(qt, k, vt)


def _dil_kernel(q_ref, kp_ref, kc_ref, kn_ref, vp_ref, vc_ref, vn_ref, o_ref, lse_ref,
                qs, ks, vs, os_, ls_, *split, dil, group, seq_len):
    halo = N_SIDE * dil
    n_blk = DIL_TILE // (Q_BLK * dil)
    sc_len = seq_len // dil
    i0 = pl.program_id(1) * (DIL_TILE // dil)

    for pair in range(2):
        cs = slice(pair * LANES, (pair + 1) * LANES)
        qs[pair] = q_ref[0, :, cs].astype(F32)
        for dst, (p_ref, c_ref, n_ref) in ((ks, (kp_ref, kc_ref, kn_ref)), (vs, (vp_ref, vc_ref, vn_ref))):
            dst[pair, 0:halo] = p_ref[0, :, cs].astype(F32)
            dst[pair, halo:halo + DIL_TILE] = c_ref[0, :, cs].astype(F32)
            dst[pair, halo + DIL_TILE:] = n_ref[0, :, cs].astype(F32)

    step = dil // DIL_SPLIT if split else dil
    if split:
        for src, dst in zip((qs, ks, vs), split):
            for pair in range(2):
                for a in range(DIL_SPLIT):
                    dst[pair, a] = src[pair, pl.ds(a, src.shape[1] // DIL_SPLIT, stride=DIL_SPLIT), :]

    def strided(src_idx, pair, r, first, n):
        if split:
            a, b = r % DIL_SPLIT, r // DIL_SPLIT
            return split[src_idx][pair, a, pl.ds(first * step + b, n, stride=step), :]
        src = (qs, ks, vs)[src_idx]
        rows = pl.ds(first * dil + r, n, stride=dil) if dil > 1 else pl.ds(first + r, n)
        return src[pair, rows, :]

    col = lax.broadcasted_iota(jnp.int32, (Q_BLK, K_WIN), 1)
    rowi = lax.broadcasted_iota(jnp.int32, (Q_BLK, K_WIN), 0)
    rel = col - N_SIDE - rowi
    absrel = jnp.abs(rel).astype(F32)
    band = jnp.abs(rel) <= N_SIDE
    lane = lax.broadcasted_iota(jnp.int32, (Q_BLK, LANES), 1)
    first_head = lane < DIL_HEAD_DIM
    scale = DIL_HEAD_DIM ** -0.5
    slopes = [2.0 ** (-8.0 * (group * DIL_HEADS_PER_GROUP + h + 1) / DIL_HEADS) for h in range(DIL_HEADS_PER_GROUP)]
    bias = [jnp.where(band, -(absrel * (slope * dil)), NEG) for slope in slopes]

    for jb in range(n_blk):
        kc0 = i0 + (jb * Q_BLK - N_SIDE)
        in_seq = (col >= -kc0) & (col < sc_len - kc0) if jb in (0, n_blk - 1) else None
        for r in range(dil):
            start = jb * Q_BLK * dil + r
            q_rows = pl.ds(start, Q_BLK, stride=dil) if dil > 1 else pl.ds(start, Q_BLK)
            for pair in range(2):
                qp = strided(0, pair, r, jb * Q_BLK, Q_BLK) * scale
                kp = strided(1, pair, r, jb * Q_BLK, K_WIN).astype(BF16)
                vp = strided(2, pair, r, jb * Q_BLK, K_WIN).astype(BF16)
                o_acc = None
                lse_acc = None
                for hh in range(2):
                    sel = first_head if hh == 0 else jnp.logical_not(first_head)
                    qm = jnp.where(sel, qp, 0.0).astype(BF16)
                    s = _dot_nt(qm, kp) + bias[pair * 2 + hh]
                    if in_seq is not None:
                        s = jnp.where(in_seq, s, NEG)
                    m = jnp.max(s, axis=1, keepdims=True)
                    p = jnp.exp(s - m)
                    l = jnp.sum(p, axis=1, keepdims=True)
                    o = _dot(p.astype(BF16), vp) * (1.0 / l)
                    lse = jnp.broadcast_to(m + jnp.log(l), (Q_BLK, LANES))
                    o_acc = o if hh == 0 else jnp.where(sel, o, o_acc)
                    lse_acc = lse if hh == 0 else jnp.where(sel, lse, lse_acc)
                os_[pair, q_rows, :] = o_acc
                ls_[pair, q_rows, :] = lse_acc

    for pair in range(2):
        cs = slice(pair * LANES, (pair + 1) * LANES)
        o_ref[0, :, cs] = os_[pair]
        lse_ref[0, :, cs] = ls_[pair]


def _dil_call(zd, group, dil):
    B, S, _ = zd.shape
    halo = N_SIDE * dil
    per = DIL_TILE // halo
    n_halo = S // halo
    g = group
    cur = lambda c: pl.BlockSpec((1, DIL_TILE, DIL_W), lambda b, i: (b, i, c))
    prev = lambda c: pl.BlockSpec((1, halo, DIL_W), lambda b, i: (b, jnp.maximum(i * per - 1, 0), c))
    nxt = lambda c: pl.BlockSpec((1, halo, DIL_W), lambda b, i: (b, jnp.minimum((i + 1) * per, n_halo - 1), c))
    n_g = len(DIL_CONFIGS)
    out_spec = pl.BlockSpec((1, DIL_TILE, DIL_W), lambda b, i: (b, i, 0))
    return pl.pallas_call(
        functools.partial(_dil_kernel, dil=dil, group=group, seq_len=S),
        grid=(B, S // DIL_TILE),
        in_specs=[cur(g), prev(n_g + g), cur(n_g + g), nxt(n_g + g),
                  prev(2 * n_g + g), cur(2 * n_g + g), nxt(2 * n_g + g)],
        out_specs=[out_spec, out_spec],
        out_shape=[jax.ShapeDtypeStruct((B, S, DIL_W), F32), jax.ShapeDtypeStruct((B, S, DIL_W), F32)],
        scratch_shapes=[
            pltpu.VMEM((2, DIL_TILE, LANES), F32),
            pltpu.VMEM((2, DIL_TILE + 2 * halo, LANES), F32),
            pltpu.VMEM((2, DIL_TILE + 2 * halo, LANES), F32),
            pltpu.VMEM((2, DIL_TILE, LANES), F32),
            pltpu.VMEM((2, DIL_TILE, LANES), F32),
        ] + ([pltpu.VMEM((2, DIL_SPLIT, rows // DIL_SPLIT, LANES), F32)
              for rows in (DIL_TILE, DIL_TILE + 2 * halo, DIL_TILE + 2 * halo)] if dil > DIL_SPLIT else []),
        compiler_params=_params(("parallel", "parallel")),
        name=f"dil_attn_d{dil}",
    )(zd, zd, zd, zd, zd, zd, zd)


def _out_kernel(x_ref, mod_ref, g1_ref, wg_ref, oa_ref, o0_ref, o1_ref, o2_ref, l0_ref, l1_ref, l2_ref,
                pa_ref, pb_ref, wo_ref, x1_ref):
    x = x_ref[0]
    mod = mod_ref[0]
    h = _rms(x, g1_ref[...]) * (1.0 + mod[1:2]) + mod[0:1]
    hb = h.astype(BF16)
    l0, l1, l2 = l0_ref[0], l1_ref[0], l2_ref[0]
    lm = jnp.maximum(jnp.maximum(l0, l1), l2)
    e0, e1, e2 = jnp.exp(l0 - lm), jnp.exp(l1 - lm), jnp.exp(l2 - lm)
    ob = (e0 * o0_ref[0] + e1 * o1_ref[0] + e2 * o2_ref[0]) / (e0 + e1 + e2)
    merged = jax.nn.sigmoid(_dot(hb, wg_ref[:, :D_MODEL])) * _dot(oa_ref[0], pa_ref[...])
    merged = merged + jax.nn.sigmoid(_dot(hb, wg_ref[:, D_MODEL:])) * _dot(ob.astype(BF16), pb_ref[...])
    x1_ref[0] = x + mod[2:3] * _dot(merged.astype(BF16), wo_ref[...])


def _out_call(x, mod, w, oa, dil_outs):
    B, S, _ = x.shape
    T = ROW_TILE
    row = lambda c: pl.BlockSpec((1, T, c), lambda b, i: (b, i, 0))
    os_ = [o for o, _ in dil_outs]
    ls_ = [l for _, l in dil_outs]
    return pl.pallas_call(
        _out_kernel,
        grid=(B, S // T),
        in_specs=[
            row(D_MODEL),
            pl.BlockSpec((1, 8, D_MODEL), lambda b, i: (b, 0, 0)),
            _const_spec((1, D_MODEL)),
            _const_spec((D_MODEL, 2 * D_MODEL)),
            row(MLA_HEADS * V_DIM),
            row(DIL_W), row(DIL_W), row(DIL_W), row(DIL_W), row(DIL_W), row(DIL_W),
            _const_spec((MLA_HEADS * V_DIM, D_MODEL)),
            _const_spec((DIL_W, D_MODEL)),
            _const_spec((D_MODEL, D_MODEL)),
        ],
        out_specs=row(D_MODEL),
        out_shape=jax.ShapeDtypeStruct((B, S, D_MODEL), F32),
        compiler_params=_params(("parallel", "parallel")),
        name="out_proj",
    )(x, mod, w["g1"], w["wg"], oa, *os_, *ls_, w["pa"], w["pb"], w["wo"])


def _ffn_kernel(xc_ref, xp_ref, xn_ref, mod_ref, g2_ref, wu_ref, cw_ref, cb_ref, wdn_ref, gf_ref, y_ref,
                h2s, a_s):
    T = xc_ref.shape[1]
    H = FFN_HALO
    mod = mod_ref[0]
    g2 = g2_ref[...]
    i = pl.program_id(1)
    n = pl.num_programs(1)

    def modnorm(v):
        return _rms(v, g2) * (1.0 + mod[4:5]) + mod[3:4]

    x1 = xc_ref[0]
    h2s[0:H] = jnp.where(i > 0, modnorm(xp_ref[0]), 0.0).astype(BF16)
    h2s[H:H + T] = modnorm(x1).astype(BF16)
    h2s[H + T:] = jnp.where(i < n - 1, modnorm(xn_ref[0]), 0.0).astype(BF16)

    rows = T + 2 * H
    for c in range(D_FF // FF_CHUNK):
        cs = slice(c * FF_CHUNK, (c + 1) * FF_CHUNK)
        ul = _dot(h2s[...], wu_ref[:, cs])
        gv = _dot(h2s[H:H + T], wu_ref[:, D_FF + c * FF_CHUNK:D_FF + (c + 1) * FF_CHUNK])
        u = (cb_ref[:, cs]
             + pltpu.roll(ul, 1, 0)[H:H + T] * cw_ref[0:1, cs]
             + ul[H:H + T] * cw_ref[1:2, cs]
             + pltpu.roll(ul, rows - 1, 0)[H:H + T] * cw_ref[2:3, cs])
        a_s[:, cs] = (jax.nn.gelu(u, approximate=True) * gv).astype(BF16)

    x2 = x1 + mod[5:6] * _dot(a_s[...], wdn_ref[...])
    y_ref[0] = _rms(x2, gf_ref[...])


def _ffn_call(x1, mod, w):
    B, S, _ = x1.shape
    T = ROW_TILE
    H = FFN_HALO
    per = T // H
    n_h = S // H
    return pl.pallas_call(
        _ffn_kernel,
        grid=(B, S // T),
        in_specs=[
            pl.BlockSpec((1, T, D_MODEL), lambda b, i: (b, i, 0)),
            pl.BlockSpec((1, H, D_MODEL), lambda b, i: (b, jnp.maximum(i * per - 1, 0), 0)),
            pl.BlockSpec((1, H, D_MODEL), lambda b, i: (b, jnp.minimum((i + 1) * per, n_h - 1), 0)),
            pl.BlockSpec((1, 8, D_MODEL), lambda b, i: (b, 0, 0)),
            _const_spec((1, D_MODEL)),
            _const_spec((D_MODEL, 2 * D_FF)),
            _const_spec((3, D_FF)),
            _const_spec((1, D_FF)),
            _const_spec((D_FF, D_MODEL)),
            _const_spec((1, D_MODEL)),
        ],
        out_specs=pl.BlockSpec((1, T, D_MODEL), lambda b, i: (b, i, 0)),
        out_shape=jax.ShapeDtypeStruct((B, S, D_MODEL), F32),
        scratch_shapes=[pltpu.VMEM((T + 2 * H, D_MODEL), BF16), pltpu.VMEM((T, D_FF), BF16)],
        compiler_params=_params(("parallel", "parallel")),
        name="conv_ffn",
    )(x1, x1, x1, mod, w["g2"], w["wu"], w["cw"], w["cb"], w["wdn"], w["gf"])


def _rope_tables(seq_len):
    inv = ROPE_THETA ** (-jnp.arange(0, QK_ROPE, 2, dtype=F32) / QK_ROPE)
    ang = jnp.arange(seq_len, dtype=F32)[:, None] * inv[None, :]
    cos, sin = jnp.cos(ang), jnp.sin(ang)
    zeros_l = jnp.zeros((seq_len, QK_NOPE), F32)
    zeros_r = jnp.zeros((seq_len, LANES - QK_NOPE - QK_ROPE), F32)
    ones_l = jnp.ones((seq_len, QK_NOPE), F32)
    scale = (QK_NOPE + QK_ROPE) ** -0.5 * math.log2(math.e)
    t1k = jnp.concatenate([zeros_l, cos, cos, zeros_r], axis=1)
    t2 = jnp.concatenate([zeros_l, -sin, sin, zeros_r], axis=1)
    t1q = jnp.concatenate([ones_l, cos, cos, zeros_r], axis=1) * scale
    return t1q.T, (t2 * scale).T, t1k, t2


def _prep_weights(norm1_g, w_in, q_norm_g, kv_norm_g, w_uq, w_ukv, p_a, p_b, w_out, norm2_g, w_up,
                  conv_w, conv_b, w_down, normf_g):
    half = QK_ROPE // 2
    c0 = Q_LORA + KV_LORA
    kr = w_in[:, c0:c0 + QK_ROPE]
    kr_chunk = jnp.concatenate(
        [jnp.zeros((D_MODEL, QK_NOPE), F32), kr, kr[:, half:], kr[:, :half]], axis=1)
    d0 = c0 + QK_ROPE
    wa = jnp.concatenate([w_in[:, :c0], kr_chunk], axis=1)
    wd = w_in[:, d0:d0 + ZD_COLS]
    wg = w_in[:, d0 + ZD_COLS:]
    uq = w_uq.reshape(Q_LORA, MLA_HEADS, QK_NOPE + QK_ROPE)
    rope = uq[:, :, QK_NOPE:]
    wq = jnp.concatenate([uq, rope[:, :, half:], rope[:, :, :half]], axis=2).reshape(Q_LORA, -1)
    ukv = w_ukv.reshape(KV_LORA, MLA_HEADS, QK_NOPE + V_DIM)
    wk = jnp.concatenate([ukv[:, :, :QK_NOPE], jnp.zeros((KV_LORA, MLA_HEADS, HEAD_PAD - QK_NOPE), F32)],
                         axis=2).reshape(KV_LORA, -1)
    wv = ukv[:, :, QK_NOPE:].reshape(KV_LORA, -1)
    b = lambda a: a.astype(BF16)
    return dict(
        g1=norm1_g.reshape(1, -1), wa=b(wa), wd=b(wd), wg=b(wg),
        qg=q_norm_g.reshape(1, -1), kvg=kv_norm_g.reshape(1, -1),
        wqt=b(wq.T), wk=b(wk), wvt=b(wv.T), pa=b(p_a), pb=b(p_b), wo=b(w_out),
        g2=norm2_g.reshape(1, -1), wu=b(w_up), cw=conv_w, cb=conv_b.reshape(1, -1), wdn=b(w_down),
        gf=normf_g.reshape(1, -1),
    )


def _trunk(x, mod, w):
    S = x.shape[1]
    assert S % DIL_TILE == 0 and S % ROW_TILE == 0 and S % MLA_TQ == 0
    qt, k, vt, zd = _in_call(x, mod, w, _rope_tables(S))
    oa = _mla_call(qt, k, vt)
    dil_outs = [_dil_call(zd, g, dil) for g, (_, dil) in enumerate(DIL_CONFIGS)]
    x1 = _out_call(x, mod, w, oa, dil_outs)
    return _ffn_call(x1, mod, w)


def _mods(c_list, ada_w, ada_b):
    c_all = jnp.concatenate(c_list, axis=0)
    n = c_all.shape[0]
    rows = -(-n // 8) * 8
    c_all = jnp.pad(c_all, ((0, rows - n), (0, 0)))
    mod = _mod_call(c_all, ada_w, ada_b.reshape(1, -1)).reshape(rows, 6, D_MODEL)
    mod = jnp.pad(mod, ((0, 0), (0, 2), (0, 0)))
    out, o = [], 0
    for c in c_list:
        out.append(mod[o:o + c.shape[0]])
        o += c.shape[0]
    return out


def kernel(x_prompt, x_sample, c_prompt, c_sample, ada_w, ada_b, norm1_g, w_in, q_norm_g, kv_norm_g, w_uq, w_ukv, p_a, p_b, w_out, norm2_g, w_up, conv_w, conv_b, w_down, normf_g):
    assert ada_w.shape[0] == 1, "single layer"
    w = _prep_weights(norm1_g[0], w_in[0], q_norm_g[0], kv_norm_g[0], w_uq[0], w_ukv[0], p_a[0], p_b[0],
                      w_out[0], norm2_g[0], w_up[0], conv_w[0], conv_b[0], w_down[0], normf_g)
    mod_p, mod_s = _mods([c_prompt, c_sample], ada_w[0], ada_b[0])
    return (_trunk(x_prompt, mod_p, w), _trunk(x_sample, mod_s, w))
```
